```python
import jax, jax.numpy as jnp
from jax import lax
import numpy as np

D_MODEL = 1024
BATCH = 2
SEQ = 8192
DEPTH = 1

D_PLE = 256
D_MIX = 2 * D_MODEL
ATT_HEADS = 16
ATT_KV_HEADS = 2
ATT_HEAD_DIM = 64
ATT_WIDTH = ATT_HEADS * ATT_HEAD_DIM
ATT_KV_WIDTH = ATT_KV_HEADS * ATT_HEAD_DIM
WINDOW = 128
BLOCK = 128
ROPE_DIM = ATT_HEAD_DIM // 4
ROPE_THETA = 500000.0
GLA_WIDTH = D_MIX - ATT_WIDTH
GLA_HEADS = 4
GLA_V_DIM = GLA_WIDTH // GLA_HEADS
GLA_K_DIM = GLA_V_DIM // 2
GLA_QK_WIDTH = GLA_HEADS * GLA_K_DIM
GLA_GATE_RANK = 16
GLA_GATE_NORMALIZER = 16.0
GLA_CHUNK = 64
EPS = 1e-6
SPLITS = (ATT_WIDTH, ATT_KV_WIDTH, ATT_KV_WIDTH,
          GLA_QK_WIDTH, GLA_QK_WIDTH, GLA_WIDTH,
          GLA_GATE_RANK,
          ATT_WIDTH, GLA_WIDTH)
D_IN_PROJ = sum(SPLITS)

kernel_name = 'hymba_swa_sink_gla_ple_block'


def rmsnorm(x, g):
    xf = x.astype(jnp.float32)
    y = xf * lax.rsqrt(jnp.mean(xf * xf, axis=-1, keepdims=True) + EPS)
    return (y * g.astype(jnp.float32)).astype(x.dtype)


def partial_rope(x, positions):
    half = ROPE_DIM // 2
    inv_freq = ROPE_THETA ** (-jnp.arange(0, ROPE_DIM, 2, dtype=jnp.float32) / ROPE_DIM)
    ang = positions.astype(jnp.float32)[..., None] * inv_freq
    cos = jnp.cos(ang)[:, :, None, :].astype(x.dtype)
    sin = jnp.sin(ang)[:, :, None, :].astype(x.dtype)
    x1 = x[..., :half]
    x2 = x[..., half:ROPE_DIM]
    return jnp.concatenate([x1 * cos - x2 * sin, x2 * cos + x1 * sin, x[..., ROPE_DIM:]], axis=-1)


def swa_sink_attention(q, k, v, sinks):
    B, S, H, dh = q.shape
    nb = S // BLOCK
    G = H // ATT_KV_HEADS
    qb = q.reshape(B, nb, BLOCK, ATT_KV_HEADS, G, dh)
    kb = k.reshape(B, nb, BLOCK, ATT_KV_HEADS, dh)
    vb = v.reshape(B, nb, BLOCK, ATT_KV_HEADS, dh)
    pad = jnp.zeros_like(kb[:, :1])
    kk = jnp.concatenate([jnp.concatenate([pad, kb[:, :-1]], axis=1), kb], axis=2)
    vv = jnp.concatenate([jnp.concatenate([pad, vb[:, :-1]], axis=1), vb], axis=2)
    s = jnp.einsum('bnqhgd,bnkhd->bnhgqk', qb, kk).astype(jnp.float32) * (dh ** -0.5)
    qi = jnp.arange(BLOCK)[:, None]
    kj = jnp.arange(2 * BLOCK)[None, :]
    diff = qi + BLOCK - kj
    band = (diff >= 0) & (diff < WINDOW)
    blk = jnp.arange(nb)[:, None, None]
    mask = band[None] & ((blk > 0) | (kj >= BLOCK)[None])
    s = jnp.where(mask[None, :, None, None], s, -jnp.inf)
    sink = sinks.astype(jnp.float32).reshape(ATT_KV_HEADS, G)[None, None, :, :, None, None]
    m = jnp.maximum(jnp.max(s, axis=-1, keepdims=True), sink)
    e = jnp.exp(s - m)
    pr = (e / (jnp.sum(e, axis=-1, keepdims=True) + jnp.exp(sink - m))).astype(v.dtype)
    o = jnp.einsum('bnhgqk,bnkhd->bnqhgd', pr, vv)
    return o.reshape(B, S, H * dh)


def gla_chunked(q, k, v, log_g):
    B, S, H, dk = q.shape
    dv = v.shape[-1]
    nc = S // GLA_CHUNK
    f = lambda t: t.astype(jnp.float32).reshape(B, nc, GLA_CHUNK, H, t.shape[-1])
    qf = f(q) * (dk ** -0.5)
    kf = f(k)
    vf = f(v)
    b = jnp.cumsum(f(log_g), axis=2)
    b_last = b[:, :, -1]
    q_dec = qf * jnp.exp(b)
    k_inv = kf * jnp.exp(-b)
    k_tail = kf * jnp.exp(b_last[:, :, None] - b)
    causal = jnp.tril(jnp.ones((GLA_CHUNK, GLA_CHUNK), dtype=bool))
    a = jnp.where(causal, jnp.einsum('bnihd,bnjhd->bnhij', q_dec, k_inv), 0.0)
    o_intra = jnp.einsum('bnhij,bnjhv->bnihv', a, vf)
    kv = jnp.einsum('bnjhd,bnjhv->bnhdv', k_tail, vf)
    decay = jnp.exp(b_last)

    def step(state, inp):
        dec, kvc = inp
        return dec[..., None] * state + kvc, state

    init = jnp.zeros((B, H, dk, dv), jnp.float32)
    _, s_prev = lax.scan(step, init, (jnp.moveaxis(decay, 1, 0), jnp.moveaxis(kv, 1, 0)))
    s_prev = jnp.moveaxis(s_prev, 0, 1)
    o_inter = jnp.einsum('bnihd,bnhdv->bnihv', q_dec, s_prev)
    return (o_intra + o_inter).reshape(B, S, H, dv)


def setup_inputs(seed: int = 0) -> dict:
    key = jax.random.key(seed)
    ks = jax.random.split(key, 16)
    nrm = lambda k, shape, scale: jax.random.normal(k, shape, jnp.float32) * scale
    x = nrm(ks[0], (BATCH, SEQ, D_MODEL), 1.0)
    p = nrm(ks[1], (DEPTH, BATCH, SEQ, D_PLE), 1.0)
    positions = jnp.broadcast_to(jnp.arange(SEQ, dtype=jnp.int32), (BATCH, SEQ))
    return {
        'x': x,
        'p': p,
        'positions': positions,
        'norm_mix': 1.0 + nrm(ks[2], (DEPTH, D_MODEL), 0.02),
        'w_in': nrm(ks[3], (DEPTH, D_MODEL, D_IN_PROJ), D_MODEL ** -0.5),
        'attn_sinks': nrm(ks[4], (DEPTH, ATT_HEADS), 0.5),
        'w_gate_up': nrm(ks[5], (DEPTH, GLA_GATE_RANK, GLA_QK_WIDTH), GLA_GATE_RANK ** -0.5),
        'b_gate': nrm(ks[6], (DEPTH, GLA_QK_WIDTH), 0.1),
        'gla_norm': 1.0 + nrm(ks[7], (DEPTH, GLA_V_DIM), 0.02),
        'w_out': nrm(ks[8], (DEPTH, D_MIX, D_MODEL), D_MIX ** -0.5),
        'ple_norm': 1.0 + nrm(ks[9], (DEPTH, D_MODEL), 0.02),
        'w_ple_gate': nrm(ks[10], (DEPTH, D_MODEL, D_MODEL), D_MODEL ** -0.5),
        'w_ple_proj': nrm(ks[11], (DEPTH, D_PLE, D_MODEL), D_PLE ** -0.5),
        'final_norm': 1.0 + nrm(ks[12], (D_MODEL,), 0.02),
    }


def reference(x, p, positions, norm_mix, w_in, attn_sinks, w_gate_up, b_gate, gla_norm,
              w_out, ple_norm, w_ple_gate, w_ple_proj, final_norm):
    B, S, _ = x.shape
    offsets = []
    acc = 0
    for n in SPLITS[:-1]:
        acc += n
        offsets.append(acc)
    h = x
    for i in range(DEPTH):
        u = rmsnorm(h, norm_mix[i])
        z = u @ w_in[i]
        aq, ak, av, gq, gk, gv, g_low, za, zg = jnp.split(z, offsets, axis=-1)
        aq = partial_rope(aq.reshape(B, S, ATT_HEADS, ATT_HEAD_DIM), positions)
        ak = partial_rope(ak.reshape(B, S, ATT_KV_HEADS, ATT_HEAD_DIM), positions)
        av = av.reshape(B, S, ATT_KV_HEADS, ATT_HEAD_DIM)
        y_att = swa_sink_attention(aq, ak, av, attn_sinks[i])
        log_g = jax.nn.log_sigmoid((g_low @ w_gate_up[i] + b_gate[i]).astype(jnp.float32)) / GLA_GATE_NORMALIZER
        o = gla_chunked(gq.reshape(B, S, GLA_HEADS, GLA_K_DIM),
                        gk.reshape(B, S, GLA_HEADS, GLA_K_DIM),
                        gv.reshape(B, S, GLA_HEADS, GLA_V_DIM),
                        log_g.reshape(B, S, GLA_HEADS, GLA_K_DIM)).astype(x.dtype)
        y_gla = rmsnorm(o, gla_norm[i]).reshape(B, S, GLA_WIDTH)
        y = jnp.concatenate([y_att * jax.nn.silu(za), y_gla * jax.nn.silu(zg)], axis=-1) @ w_out[i]
        h = h + y
        gate = jax.nn.sigmoid(rmsnorm(h, ple_norm[i]) @ w_ple_gate[i])
        h = h + gate * (p[i] @ w_ple_proj[i])
    return rmsnorm(h, final_norm)
```

```python
import functools

import jax
import jax.numpy as jnp
import numpy as np
from jax import lax
from jax.experimental import pallas as pl
from jax.experimental.pallas import tpu as pltpu

D_MODEL = 1024
D_PLE = 256
ATT_HEADS = 16
ATT_KV_HEADS = 2
ATT_HEAD_DIM = 64
ATT_WIDTH = ATT_HEADS * ATT_HEAD_DIM
ATT_KV_WIDTH = ATT_KV_HEADS * ATT_HEAD_DIM
ATT_GROUP = ATT_HEADS // ATT_KV_HEADS
BLOCK = 128
ROPE_DIM = ATT_HEAD_DIM // 4
ROPE_HALF = ROPE_DIM // 2
ROPE_THETA = 500000.0
GLA_HEADS = 4
GLA_V_DIM = 256
GLA_K_DIM = 128
GLA_WIDTH = GLA_HEADS * GLA_V_DIM
GLA_QK_WIDTH = GLA_HEADS * GLA_K_DIM
GLA_GATE_RANK = 16
GLA_GATE_NORMALIZER = 16.0
GLA_CHUNK = 64
EPS = 1e-6

LANES = 128
SUBLANES = 8
VMEM_LIMIT = 56 * 1024 * 1024

C_Q = 0
C_KV = C_Q + ATT_WIDTH
C_GQ = C_KV + 4 * ATT_KV_WIDTH
C_GK = C_GQ + GLA_QK_WIDTH
C_GV = C_GK + GLA_QK_WIDTH
C_GL = C_GV + GLA_WIDTH
C_ZA = C_GL + LANES
C_ZG = C_ZA + ATT_WIDTH
C_END = C_ZG + GLA_WIDTH

TM_IN = 512
TM_OUT = 512
TS_GLA = 512
CUM_BLK = 256

BF16 = jnp.bfloat16
F32 = jnp.float32


def _dot(a, b):
    return jnp.dot(a, b, preferred_element_type=F32)


def _dot_nt(a, b):
    return lax.dot_general(a, b, (((1,), (1,)), ((), ())), preferred_element_type=F32)


def _dot_tn(a, b):
    return lax.dot_general(a, b, (((0,), (0,)), ((), ())), preferred_element_type=F32)


def _rms(x, g):
    return x * lax.rsqrt(jnp.mean(x * x, axis=-1, keepdims=True) + EPS) * g


def _silu(z):
    return z * (1.0 / (1.0 + jnp.exp(-z)))


def _inproj_kernel(x_ref, pos_ref, nmix_ref, w_ref, wup_ref, bg_ref, rope_ref, ltri_ref,
                   q_ref, kv_ref, qd_ref, ki_ref, kt_ref, gv_ref, bl_ref, sg_ref):
    u = _rms(x_ref[...], nmix_ref[...]).astype(BF16)

    ang = pos_ref[...].astype(F32) * rope_ref[0:1, :]
    cos_t = jnp.cos(ang)
    sin_t = jnp.sin(ang)
    sin_a = sin_t * rope_ref[1:2, :]
    sin_b = sin_t * rope_ref[2:3, :]

    def rope(z, scale):
        up = pltpu.roll(z, LANES - ROPE_HALF, 1)
        dn = pltpu.roll(z, ROPE_HALF, 1)
        out = z * cos_t + up * sin_a + dn * sin_b
        return out * scale if scale != 1.0 else out

    zq = _dot(u, w_ref[:, C_Q:C_KV])
    for c in range(ATT_WIDTH // LANES):
        q_ref[:, c * LANES:(c + 1) * LANES] = rope(
            zq[:, c * LANES:(c + 1) * LANES], ATT_HEAD_DIM ** -0.5).astype(BF16)

    zkv = _dot(u, w_ref[:, C_KV:C_GQ])
    for c in range(2):
        kv_ref[:, c * LANES:(c + 1) * LANES] = rope(zkv[:, c * LANES:(c + 1) * LANES], 1.0).astype(BF16)
    kv_ref[:, 2 * LANES:] = zkv[:, 2 * LANES:].astype(BF16)

    gv_ref[...] = _dot(u, w_ref[:, C_GV:C_GL]).astype(BF16)
    sg_ref[...] = _silu(_dot(u, w_ref[:, C_ZA:C_END])).astype(BF16)

    glow = _dot(u, w_ref[:, C_GL:C_ZA]).astype(BF16)
    logit = _dot(glow, wup_ref[...]) + bg_ref[...]
    lg = (jnp.minimum(logit, 0.0) - jnp.log1p(jnp.exp(-jnp.abs(logit)))) * (1.0 / GLA_GATE_NORMALIZER)
    lg_hi = lg.astype(BF16)
    lg_lo = (lg - lg_hi.astype(F32)).astype(BF16)
    tm = lg.shape[0]
    ltri = ltri_ref[...]
    b = jnp.concatenate(
        [_dot(ltri, lg_hi[r:r + CUM_BLK]) + _dot(ltri, lg_lo[r:r + CUM_BLK])
         for r in range(0, tm, CUM_BLK)], axis=0)
    nch = tm // GLA_CHUNK
    b3 = b.reshape(nch, GLA_CHUNK, GLA_QK_WIDTH)
    b_last = b3[:, GLA_CHUNK - 1:GLA_CHUNK, :]
    r3 = b_last - b3
    bl_ref[...] = b_last.reshape(nch, GLA_QK_WIDTH)

    gq = _dot(u, w_ref[:, C_GQ:C_GK]) * (GLA_K_DIM ** -0.5)
    qd_ref[...] = (gq * jnp.exp(b)).astype(BF16)
    gk = _dot(u, w_ref[:, C_GK:C_GV])
    ki_ref[...] = (gk * jnp.exp(-b)).astype(BF16)
    kt_ref[...] = (gk * jnp.exp(r3.reshape(tm, GLA_QK_WIDTH))).astype(BF16)


def _attn_kernel(sinks_ref, q_ref, kvc_ref, kvp_ref, sa_ref, o_ref):
    blk = pl.program_id(1)
    prev_bias = jnp.where(blk > 0, 0.0, -jnp.inf).astype(F32)
    row = lax.broadcasted_iota(jnp.int32, (BLOCK, BLOCK), 0)
    col = lax.broadcasted_iota(jnp.int32, (BLOCK, BLOCK), 1)
    causal = col <= row
    lo_half = col < ATT_HEAD_DIM
    zero = jnp.zeros((), BF16)

    for g in range(ATT_KV_HEADS):
        kcat = jnp.concatenate([kvp_ref[:, g * LANES:(g + 1) * LANES],
                                kvc_ref[:, g * LANES:(g + 1) * LANES]], axis=0)
        vcat = jnp.concatenate([kvp_ref[:, (2 + g) * LANES:(3 + g) * LANES],
                                kvc_ref[:, (2 + g) * LANES:(3 + g) * LANES]], axis=0)
        parts = []
        for pr in range(ATT_GROUP // 2):
            c0 = (g * (ATT_GROUP // 2) + pr) * LANES
            qp = q_ref[:, c0:c0 + LANES]
            parts.append(jnp.where(lo_half, qp, zero))
            parts.append(jnp.where(lo_half, zero, qp))
        qstack = jnp.concatenate(parts, axis=0)
        s = _dot_nt(qstack, kcat)

        pcat = []
        dens = []
        for h in range(ATT_GROUP):
            sh = s[h * BLOCK:(h + 1) * BLOCK]
            sf = jnp.where(causal, sh[:, BLOCK:], sh[:, :BLOCK] + prev_bias)
            sink = sinks_ref[g * ATT_GROUP + h]
            m = jnp.maximum(jnp.max(sf, axis=-1, keepdims=True), sink)
            e = jnp.exp(sf - m)
            dens.append(jnp.sum(e, axis=-1, keepdims=True) + jnp.exp(sink - m))
            pcat.append(jnp.concatenate([jnp.where(causal, 0.0, e), jnp.where(causal, e, 0.0)],
                                        axis=1).astype(BF16))
        o = _dot(jnp.concatenate(pcat, axis=0), vcat)
        for pr in range(ATT_GROUP // 2):
            he, ho = 2 * pr, 2 * pr + 1
            oe = o[he * BLOCK:(he + 1) * BLOCK] * (1.0 / dens[he])
            oo = o[ho * BLOCK:(ho + 1) * BLOCK] * (1.0 / dens[ho])
            c0 = (g * (ATT_GROUP // 2) + pr) * LANES
            y = jnp.where(lo_half, oe, oo) * sa_ref[:, c0:c0 + LANES].astype(F32)
            o_ref[:, c0:c0 + LANES] = y.astype(BF16)


def _gla_kernel(qd_ref, ki_ref, kt_ref, v_ref, bl_ref, sg_ref, gn_ref, o_ref, st_ref):
    @pl.when(pl.program_id(2) == 0)
    def _():
        st_ref[...] = jnp.zeros_like(st_ref)

    row = lax.broadcasted_iota(jnp.int32, (GLA_CHUNK, GLA_CHUNK), 0)
    col = lax.broadcasted_iota(jnp.int32, (GLA_CHUNK, GLA_CHUNK), 1)
    tril = col <= row
    gn = gn_ref[...]
    decay_all = jnp.exp(bl_ref[...])
    for c in range(qd_ref.shape[0] // GLA_CHUNK):
        sl = slice(c * GLA_CHUNK, (c + 1) * GLA_CHUNK)
        qd = qd_ref[sl, :]
        v = v_ref[sl, :]
        st = st_ref[...]
        a = jnp.where(tril, _dot_nt(qd, ki_ref[sl, :]), 0.0).astype(BF16)
        o = _dot(a, v) + _dot_nt(qd, st.astype(BF16))
        st_ref[...] = st * decay_all[c:c + 1, :] + _dot_tn(v, kt_ref[sl, :])
        y = _rms(o, gn) * sg_ref[sl, :].astype(F32)
        o_ref[sl, :] = y.astype(BF16)


def _out_kernel(x_ref, ya_ref, yg_ref, p_ref, woa_ref, wog_ref, pn_ref, wpg_ref, wpp_ref, fn_ref, o_ref):
    h = x_ref[...] + _dot(ya_ref[...], woa_ref[...]) + _dot(yg_ref[...], wog_ref[...])
    gate_in = _rms(h, pn_ref[...]).astype(BF16)
    gate = 1.0 / (1.0 + jnp.exp(-_dot(gate_in, wpg_ref[...])))
    h = h + gate * _dot(p_ref[...].astype(BF16), wpp_ref[...])
    o_ref[...] = _rms(h, fn_ref[...])


def _const_spec(shape):
    nd = len(shape)
    return pl.BlockSpec(shape, lambda *_: (0,) * nd, pipeline_mode=pl.Buffered(1))


def _rope_rows():
    lane = np.arange(LANES) % ATT_HEAD_DIM
    inv_freq = ROPE_THETA ** (-np.arange(0, ROPE_DIM, 2, dtype=np.float32) / ROPE_DIM)
    rows = np.zeros((SUBLANES, LANES), np.float32)
    rows[0] = np.where(lane < ROPE_DIM, inv_freq[lane % ROPE_HALF], 0.0)
    rows[1] = np.where(lane < ROPE_HALF, -1.0, 0.0)
    rows[2] = np.where((lane >= ROPE_HALF) & (lane < ROPE_DIM), 1.0, 0.0)
    return rows


def _chunk_tril(n):
    i = np.arange(n)
    same = (i[:, None] // GLA_CHUNK) == (i[None, :] // GLA_CHUNK)
    return (same & (i[None, :] <= i[:, None])).astype(np.float32)


def _layer(h, p, pos, norm_mix, w_in, sinks, w_gate_up, b_gate, gla_norm, w_out, ple_norm,
           w_ple_gate, w_ple_proj, out_norm, batch, seq):
    n = batch * seq
    o = np.cumsum((0, ATT_WIDTH, ATT_KV_WIDTH, ATT_KV_WIDTH, GLA_QK_WIDTH, GLA_QK_WIDTH, GLA_WIDTH,
                   GLA_GATE_RANK, ATT_WIDTH, GLA_WIDTH))
    hd = ATT_HEAD_DIM
    k0, k1 = w_in[:, o[1]:o[1] + hd], w_in[:, o[1] + hd:o[2]]
    v0, v1 = w_in[:, o[2]:o[2] + hd], w_in[:, o[2] + hd:o[3]]
    w_r = jnp.concatenate(
        [w_in[:, o[0]:o[1]], k0, k0, k1, k1, v0, v0, v1, v1,
         w_in[:, o[3]:o[6]],
         w_in[:, o[6]:o[7]], jnp.zeros((D_MODEL, LANES - GLA_GATE_RANK), w_in.dtype),
         w_in[:, o[7]:o[9]]], axis=1).astype(BF16)
    wup = jnp.concatenate([w_gate_up, jnp.zeros((LANES - GLA_GATE_RANK, GLA_QK_WIDTH), w_gate_up.dtype)],
                          axis=0).astype(BF16)
    rope_rows = jnp.asarray(_rope_rows())
    ltri = jnp.asarray(_chunk_tril(CUM_BLK), dtype=BF16)

    row = lambda i: (i, 0)
    params = lambda sem: pltpu.CompilerParams(dimension_semantics=sem, vmem_limit_bytes=VMEM_LIMIT)

    q, kv, qd, ki, kt, gv, bl, sg = pl.pallas_call(
        _inproj_kernel,
        grid=(n // TM_IN,),
        in_specs=[
            pl.BlockSpec((TM_IN, D_MODEL), row),
            pl.BlockSpec((TM_IN, 1), row),
            _const_spec((1, D_MODEL)),
            _const_spec((D_MODEL, C_END)),
            _const_spec((LANES, GLA_QK_WIDTH)),
            _const_spec((1, GLA_QK_WIDTH)),
            _const_spec((SUBLANES, LANES)),
            _const_spec((CUM_BLK, CUM_BLK)),
        ],
        out_specs=[
            pl.BlockSpec((TM_IN, ATT_WIDTH), row),
            pl.BlockSpec((TM_IN, 4 * ATT_KV_WIDTH), row),
            pl.BlockSpec((TM_IN, GLA_QK_WIDTH), row),
            pl.BlockSpec((TM_IN, GLA_QK_WIDTH), row),
            pl.BlockSpec((TM_IN, GLA_QK_WIDTH), row),
            pl.BlockSpec((TM_IN, GLA_WIDTH), row),
            pl.BlockSpec((TM_IN // GLA_CHUNK, GLA_QK_WIDTH), row),
            pl.BlockSpec((TM_IN, ATT_WIDTH + GLA_WIDTH), row),
        ],
        out_shape=[
            jax.ShapeDtypeStruct((n, ATT_WIDTH), BF16),
            jax.ShapeDtypeStruct((n, 4 * ATT_KV_WIDTH), BF16),
            jax.ShapeDtypeStruct((n, GLA_QK_WIDTH), BF16),
            jax.ShapeDtypeStruct((n, GLA_QK_WIDTH), BF16),
            jax.ShapeDtypeStruct((n, GLA_QK_WIDTH), BF16),
            jax.ShapeDtypeStruct((n, GLA_WIDTH), BF16),
            jax.ShapeDtypeStruct((n // GLA_CHUNK, GLA_QK_WIDTH), F32),
            jax.ShapeDtypeStruct((n, ATT_WIDTH + GLA_WIDTH), BF16),
        ],
        compiler_params=params(("parallel",)),
        name="inproj",
    )(h, pos.reshape(n, 1), norm_mix.reshape(1, D_MODEL), w_r, wup, b_gate.reshape(1, GLA_QK_WIDTH),
      rope_rows, ltri)

    nb = seq // BLOCK
    ya = pl.pallas_call(
        _attn_kernel,
        grid=(batch, nb),
        in_specs=[
            pl.BlockSpec(memory_space=pltpu.SMEM),
            pl.BlockSpec((BLOCK, ATT_WIDTH), lambda b, i: (b * nb + i, 0)),
            pl.BlockSpec((BLOCK, 4 * ATT_KV_WIDTH), lambda b, i: (b * nb + i, 0)),
            pl.BlockSpec((BLOCK, 4 * ATT_KV_WIDTH), lambda b, i: (b * nb + jnp.maximum(i - 1, 0), 0)),
            pl.BlockSpec((BLOCK, ATT_WIDTH), lambda b, i: (b * nb + i, 0)),
        ],
        out_specs=pl.BlockSpec((BLOCK, ATT_WIDTH), lambda b, i: (b * nb + i, 0)),
        out_shape=jax.ShapeDtypeStruct((n, ATT_WIDTH), BF16),
        compiler_params=params(("parallel", "parallel")),
        name="swa_attn",
    )(sinks.astype(F32), q, kv, kv, sg)

    nt = seq // TS_GLA
    qk_spec = pl.BlockSpec((TS_GLA, GLA_K_DIM), lambda b, hh, t: (b * nt + t, hh))
    yg = pl.pallas_call(
        _gla_kernel,
        grid=(batch, GLA_HEADS, nt),
        in_specs=[
            qk_spec, qk_spec, qk_spec,
            pl.BlockSpec((TS_GLA, GLA_V_DIM), lambda b, hh, t: (b * nt + t, hh)),
            pl.BlockSpec((TS_GLA // GLA_CHUNK, GLA_K_DIM), lambda b, hh, t: (b * nt + t, hh)),
            pl.BlockSpec((TS_GLA, GLA_V_DIM), lambda b, hh, t: (b * nt + t, ATT_WIDTH // GLA_V_DIM + hh)),
            pl.BlockSpec((1, GLA_V_DIM), lambda b, hh, t: (0, 0)),
        ],
        out_specs=pl.BlockSpec((TS_GLA, GLA_V_DIM), lambda b, hh, t: (b * nt + t, hh)),
        out_shape=jax.ShapeDtypeStruct((n, GLA_WIDTH), BF16),
        scratch_shapes=[pltpu.VMEM((GLA_V_DIM, GLA_K_DIM), F32)],
        compiler_params=params(("parallel", "parallel", "arbitrary")),
        name="gla",
    )(qd, ki, kt, gv, bl, sg, gla_norm.reshape(1, GLA_V_DIM))

    w_out_b = w_out.astype(BF16)
    return pl.pallas_call(
        _out_kernel,
        grid=(n // TM_OUT,),
        in_specs=[
            pl.BlockSpec((TM_OUT, D_MODEL), row),
            pl.BlockSpec((TM_OUT, ATT_WIDTH), row),
            pl.BlockSpec((TM_OUT, GLA_WIDTH), row),
            pl.BlockSpec((TM_OUT, D_PLE), row),
            _const_spec((ATT_WIDTH, D_MODEL)),
            _const_spec((GLA_WIDTH, D_MODEL)),
            _const_spec((1, D_MODEL)),
            _const_spec((D_MODEL, D_MODEL)),
            _const_spec((D_PLE, D_MODEL)),
            _const_spec((1, D_MODEL)),
        ],
        out_specs=pl.BlockSpec((TM_OUT, D_MODEL), row),
        out_shape=jax.ShapeDtypeStruct((n, D_MODEL), F32),
        compiler_params=params(("parallel",)),
        name="outproj",
    )(h, ya, yg, p, w_out_b[:ATT_WIDTH], w_out_b[ATT_WIDTH:], ple_norm.reshape(1, D_MODEL),
      w_ple_gate.astype(BF16), w_ple_proj.astype(BF16), out_norm.reshape(1, D_MODEL))


def kernel(x, p, positions, norm_mix, w_in, attn_sinks, w_gate_up, b_gate, gla_norm, w_out, ple_norm,
           w_ple_gate, w_ple_proj, final_norm):
    batch, seq, _ = x.shape
    depth = w_in.shape[0]
    n = batch * seq
    h = x.reshape(n, D_MODEL)
    pos = positions.reshape(n)
    for i in range(depth):
        assert depth == 1, "final norm is fused into the single layer's output kernel"
        h = _layer(h, p[i].reshape(n, D_PLE), pos, norm_mix[i], w_in[i], attn_sinks[i], w_gate_up[i],
                   b_gate[i], gla_norm[i], w_out[i], ple_norm[i], w_ple_gate[i], w_ple_proj[i],
                   final_norm, batch, seq)
    return h.reshape(batch, seq, D_MODEL)
```

```python
import jax
import jax.numpy as jnp
import numpy as np
from jax import lax
from jax.experimental import pallas as pl
from jax.experimental.pallas import tpu as pltpu

D_MODEL = 1024
D_PLE = 256
ATT_HEADS = 16
ATT_KV_HEADS = 2
ATT_HEAD_DIM = 64
ATT_WIDTH = ATT_HEADS * ATT_HEAD_DIM
ATT_KV_WIDTH = ATT_KV_HEADS * ATT_HEAD_DIM
ATT_GROUP = ATT_HEADS // ATT_KV_HEADS
BLOCK = 128
ROPE_DIM = ATT_HEAD_DIM // 4
ROPE_HALF = ROPE_DIM // 2
ROPE_THETA = 500000.0
GLA_HEADS = 4
GLA_V_DIM = 256
GLA_K_DIM = 128
GLA_WIDTH = GLA_HEADS * GLA_V_DIM
GLA_QK_WIDTH = GLA_HEADS * GLA_K_DIM
GLA_GATE_RANK = 16
GLA_GATE_NORMALIZER = 16.0
GLA_CHUNK = 64
EPS = 1e-6

LANES = 128
SUBLANES = 8
VMEM_LIMIT = 56 * 1024 * 1024

C_Q = 0
C_KV = C_Q + ATT_WIDTH
C_GQ = C_KV + 4 * ATT_KV_WIDTH
C_GK = C_GQ + GLA_QK_WIDTH
C_GV = C_GK + GLA_QK_WIDTH
C_GL = C_GV + GLA_WIDTH
C_ZA = C_GL + LANES
C_ZG = C_ZA + ATT_WIDTH
C_END = C_ZG + GLA_WIDTH

TM_MIX = 512
TM_OUT = 512
CUM_BLK = 256

BF16 = jnp.bfloat16
F32 = jnp.float32


def _dot(a, b):
    return jnp.dot(a, b, preferred_element_type=F32)


def _dot_nt(a, b):
    return lax.dot_general(a, b, (((1,), (1,)), ((), ())), preferred_element_type=F32)


def _dot_tn(a, b):
    return lax.dot_general(a, b, (((0,), (0,)), ((), ())), preferred_element_type=F32)


def _rms(x, g):
    return x * lax.rsqrt(jnp.mean(x * x, axis=-1, keepdims=True) + EPS) * g


def _sigmoid(z):
    return 0.5 * (1.0 + jnp.tanh(0.5 * z))


def _inproj(x_ref, pos_ref, nmix_ref, w_ref, wup_ref, bg_ref, rope_ref, ltri_ref,
            u_s, q_s, kv_s, qd_s, ki_s, kt_s, gv_s, sg_s, bl_s):
    tm = x_ref.shape[0]
    u_s[...] = _rms(x_ref[...], nmix_ref[...]).astype(BF16)

    ang = pos_ref[...].astype(F32) * rope_ref[0:1, :]
    cos_t = jnp.cos(ang)
    sin_t = jnp.sin(ang)
    sin_a = sin_t * rope_ref[1:2, :]
    sin_b = sin_t * rope_ref[2:3, :]

    def rope(z, scale):
        up = pltpu.roll(z, LANES - ROPE_HALF, 1)
        dn = pltpu.roll(z, ROPE_HALF, 1)
        out = z * cos_t + up * sin_a + dn * sin_b
        return out * scale if scale != 1.0 else out

    zq = _dot(u_s[...], w_ref[:, C_Q:C_KV])
    for c in range(ATT_WIDTH // LANES):
        q_s[:, c * LANES:(c + 1) * LANES] = rope(
            zq[:, c * LANES:(c + 1) * LANES], ATT_HEAD_DIM ** -0.5).astype(BF16)

    zkv = _dot(u_s[...], w_ref[:, C_KV:C_GQ])
    for c in range(2):
        kv_s[BLOCK:, c * LANES:(c + 1) * LANES] = rope(zkv[:, c * LANES:(c + 1) * LANES], 1.0).astype(BF16)
    kv_s[BLOCK:, 2 * LANES:] = zkv[:, 2 * LANES:].astype(BF16)

    gv_s[...] = _dot(u_s[...], w_ref[:, C_GV:C_GL]).astype(BF16)
    zg = _dot(u_s[...], w_ref[:, C_ZA:C_END])
    sg_s[...] = (zg * _sigmoid(zg)).astype(BF16)

    glow = _dot(u_s[...], w_ref[:, C_GL:C_ZA]).astype(BF16)
    logit = _dot(glow, wup_ref[...]) + bg_ref[...]
    lg = (jnp.minimum(logit, 0.0) - jnp.log(1.0 + jnp.exp(-jnp.abs(logit)))) * (1.0 / GLA_GATE_NORMALIZER)
    lg_hi = lg.astype(BF16)
    lg_lo = (lg - lg_hi.astype(F32)).astype(BF16)
    ltri = ltri_ref[...]
    b = jnp.concatenate(
        [_dot(ltri, lg_hi[r:r + CUM_BLK]) + _dot(ltri, lg_lo[r:r + CUM_BLK])
         for r in range(0, tm, CUM_BLK)], axis=0)
    nch = tm // GLA_CHUNK
    b3 = b.reshape(nch, GLA_CHUNK, GLA_QK_WIDTH)
    b_last = b3[:, GLA_CHUNK - 1:GLA_CHUNK, :]
    r3 = b_last - b3
    bl_s[...] = b_last.reshape(nch, GLA_QK_WIDTH)

    gq = _dot(u_s[...], w_ref[:, C_GQ:C_GK]) * (GLA_K_DIM ** -0.5)
    qd_s[...] = (gq * jnp.exp(b)).astype(BF16)
    gk = _dot(u_s[...], w_ref[:, C_GK:C_GV])
    ki_s[...] = (gk * jnp.exp(-b)).astype(BF16)
    kt_s[...] = (gk * jnp.exp(r3.reshape(tm, GLA_QK_WIDTH))).astype(BF16)


def _attn_block(j, prev_bias, sinks_ref, q_s, kv_s, sg_s, y_ref):
    row = lax.broadcasted_iota(jnp.int32, (BLOCK, BLOCK), 0)
    col = lax.broadcasted_iota(jnp.int32, (BLOCK, BLOCK), 1)
    causal = col <= row
    lo_half = col < ATT_HEAD_DIM
    zero = jnp.zeros((), BF16)
    rows = slice(j * BLOCK, (j + 1) * BLOCK)
    kv_rows = slice(j * BLOCK, (j + 2) * BLOCK)

    for g in range(ATT_KV_HEADS):
        kcat = kv_s[kv_rows, g * LANES:(g + 1) * LANES]
        vcat = kv_s[kv_rows, (2 + g) * LANES:(3 + g) * LANES]
        parts = []
        for pr in range(ATT_GROUP // 2):
            c0 = (g * (ATT_GROUP // 2) + pr) * LANES
            qp = q_s[rows, c0:c0 + LANES]
            parts.append(jnp.where(lo_half, qp, zero))
            parts.append(jnp.where(lo_half, zero, qp))
        s = _dot_nt(jnp.concatenate(parts, axis=0), kcat)

        pcat = []
        dens = []
        for h in range(ATT_GROUP):
            sh = s[h * BLOCK:(h + 1) * BLOCK]
            s_prev = sh[:, :BLOCK] if prev_bias is None else sh[:, :BLOCK] + prev_bias
            sf = jnp.where(causal, sh[:, BLOCK:], s_prev)
            sink = sinks_ref[g * ATT_GROUP + h]
            m = jnp.maximum(jnp.max(sf, axis=-1, keepdims=True), sink)
            e = jnp.exp(sf - m)
            dens.append(jnp.sum(e, axis=-1, keepdims=True) + jnp.exp(sink - m))
            pcat.append(jnp.concatenate([jnp.where(causal, 0.0, e), jnp.where(causal, e, 0.0)],
                                        axis=1).astype(BF16))
        o = _dot(jnp.concatenate(pcat, axis=0), vcat)
        for pr in range(ATT_GROUP // 2):
            he, ho = 2 * pr, 2 * pr + 1
            oe = o[he * BLOCK:(he + 1) * BLOCK] * (1.0 / dens[he])
            oo = o[ho * BLOCK:(ho + 1) * BLOCK] * (1.0 / dens[ho])
            c0 = (g * (ATT_GROUP // 2) + pr) * LANES
            y = jnp.where(lo_half, oe, oo) * sg_s[rows, c0:c0 + LANES].astype(F32)
            y_ref[rows, c0:c0 + LANES] = y.astype(BF16)


def _gla_head(h, qd_s, ki_s, kt_s, gv_s, sg_s, bl_s, gn_ref, st_s, y_ref):
    row = lax.broadcasted_iota(jnp.int32, (GLA_CHUNK, GLA_CHUNK), 0)
    col = lax.broadcasted_iota(jnp.int32, (GLA_CHUNK, GLA_CHUNK), 1)
    tril = col <= row
    gn = gn_ref[...]
    kc = slice(h * GLA_K_DIM, (h + 1) * GLA_K_DIM)
    vc = slice(h * GLA_V_DIM, (h + 1) * GLA_V_DIM)
    oc = slice(ATT_WIDTH + h * GLA_V_DIM, ATT_WIDTH + (h + 1) * GLA_V_DIM)
    decay_all = jnp.exp(bl_s[:, kc])
    for c in range(qd_s.shape[0] // GLA_CHUNK):
        sl = slice(c * GLA_CHUNK, (c + 1) * GLA_CHUNK)
        qd = qd_s[sl, kc]
        v = gv_s[sl, vc]
        st = st_s[h]
        a = jnp.where(tril, _dot_nt(qd, ki_s[sl, kc]), 0.0).astype(BF16)
        o = _dot(a, v) + _dot_nt(qd, st.astype(BF16))
        st_s[h] = st * decay_all[c:c + 1, :] + _dot_tn(v, kt_s[sl, kc])
        y = _rms(o, gn) * sg_s[sl, oc].astype(F32)
        y_ref[sl, oc] = y.astype(BF16)


def _mixer_kernel(sinks_ref, x_ref, pos_ref, nmix_ref, w_ref, wup_ref, bg_ref, rope_ref, ltri_ref, gn_ref,
                  y_ref, u_s, q_s, kv_s, qd_s, ki_s, kt_s, gv_s, sg_s, bl_s, st_s):
    t = pl.program_id(1)
    tm = x_ref.shape[0]

    @pl.when(t == 0)
    def _():
        st_s[...] = jnp.zeros_like(st_s)
        kv_s[0:BLOCK, :] = jnp.zeros((BLOCK, kv_s.shape[1]), BF16)

    _inproj(x_ref, pos_ref, nmix_ref, w_ref, wup_ref, bg_ref, rope_ref, ltri_ref,
            u_s, q_s, kv_s, qd_s, ki_s, kt_s, gv_s, sg_s, bl_s)

    first_bias = jnp.where(t > 0, 0.0, -jnp.inf).astype(F32)
    for j in range(tm // BLOCK):
        _attn_block(j, first_bias if j == 0 else None, sinks_ref, q_s, kv_s, sg_s, y_ref)
    kv_s[0:BLOCK, :] = kv_s[tm:tm + BLOCK, :]

    for h in range(GLA_HEADS):
        _gla_head(h, qd_s, ki_s, kt_s, gv_s, sg_s, bl_s, gn_ref, st_s, y_ref)


def _out_kernel(x_ref, y_ref, p_ref, wo_ref, pn_ref, wpg_ref, wpp_ref, fn_ref, o_ref):
    h = x_ref[...] + _dot(y_ref[...], wo_ref[...])
    gate = _sigmoid(_dot(_rms(h, pn_ref[...]).astype(BF16), wpg_ref[...]))
    h = h + gate * _dot(p_ref[...].astype(BF16), wpp_ref[...])
    o_ref[...] = _rms(h, fn_ref[...])


def _const_spec(shape):
    nd = len(shape)
    return pl.BlockSpec(shape, lambda *_: (0,) * nd, pipeline_mode=pl.Buffered(1))


def _rope_rows():
    lane = np.arange(LANES) % ATT_HEAD_DIM
    inv_freq = ROPE_THETA ** (-np.arange(0, ROPE_DIM, 2, dtype=np.float32) / ROPE_DIM)
    rows = np.zeros((SUBLANES, LANES), np.float32)
    rows[0] = np.where(lane < ROPE_DIM, inv_freq[lane % ROPE_HALF], 0.0)
    rows[1] = np.where(lane < ROPE_HALF, -1.0, 0.0)
    rows[2] = np.where((lane >= ROPE_HALF) & (lane < ROPE_DIM), 1.0, 0.0)
    return rows


def _chunk_tril(n):
    i = np.arange(n)
    same = (i[:, None] // GLA_CHUNK) == (i[None, :] // GLA_CHUNK)
    return (same & (i[None, :] <= i[:, None])).astype(np.float32)


def _layer(h, p, pos, norm_mix, w_in, sinks, w_gate_up, b_gate, gla_norm, w_out, ple_norm,
           w_ple_gate, w_ple_proj, out_norm, batch, seq):
    n = batch * seq
    o = np.cumsum((0, ATT_WIDTH, ATT_KV_WIDTH, ATT_KV_WIDTH, GLA_QK_WIDTH, GLA_QK_WIDTH, GLA_WIDTH,
                   GLA_GATE_RANK, ATT_WIDTH, GLA_WIDTH))
    hd = ATT_HEAD_DIM
    k0, k1 = w_in[:, o[1]:o[1] + hd], w_in[:, o[1] + hd:o[2]]
    v0, v1 = w_in[:, o[2]:o[2] + hd], w_in[:, o[2] + hd:o[3]]
    w_r = jnp.concatenate(
        [w_in[:, o[0]:o[1]], k0, k0, k1, k1, v0, v0, v1, v1,
         w_in[:, o[3]:o[6]],
         w_in[:, o[6]:o[7]], jnp.zeros((D_MODEL, LANES - GLA_GATE_RANK), w_in.dtype),
         w_in[:, o[7]:o[9]]], axis=1).astype(BF16)
    wup = jnp.concatenate([w_gate_up, jnp.zeros((LANES - GLA_GATE_RANK, GLA_QK_WIDTH), w_gate_up.dtype)],
                          axis=0).astype(BF16)
    rope_rows = jnp.asarray(_rope_rows())
    ltri = jnp.asarray(_chunk_tril(CUM_BLK), dtype=BF16)

    params = lambda sem: pltpu.CompilerParams(dimension_semantics=sem, vmem_limit_bytes=VMEM_LIMIT)

    nt = seq // TM_MIX
    tile = lambda b, t: (b * nt + t, 0)
    y = pl.pallas_call(
        _mixer_kernel,
        grid=(batch, nt),
        in_specs=[
            pl.BlockSpec(memory_space=pltpu.SMEM),
            pl.BlockSpec((TM_MIX, D_MODEL), tile),
            pl.BlockSpec((TM_MIX, 1), tile),
            _const_spec((1, D_MODEL)),
            _const_spec((D_MODEL, C_END)),
            _const_spec((LANES, GLA_QK_WIDTH)),
            _const_spec((1, GLA_QK_WIDTH)),
            _const_spec((SUBLANES, LANES)),
            _const_spec((CUM_BLK, CUM_BLK)),
            _const_spec((1, GLA_V_DIM)),
        ],
        out_specs=pl.BlockSpec((TM_MIX, ATT_WIDTH + GLA_WIDTH), tile),
        out_shape=jax.ShapeDtypeStruct((n, ATT_WIDTH + GLA_WIDTH), BF16),
        scratch_shapes=[
            pltpu.VMEM((TM_MIX, D_MODEL), BF16),
            pltpu.VMEM((TM_MIX, ATT_WIDTH), BF16),
            pltpu.VMEM((TM_MIX + BLOCK, 4 * ATT_KV_WIDTH), BF16),
            pltpu.VMEM((TM_MIX, GLA_QK_WIDTH), BF16),
            pltpu.VMEM((TM_MIX, GLA_QK_WIDTH), BF16),
            pltpu.VMEM((TM_MIX, GLA_QK_WIDTH), BF16),
            pltpu.VMEM((TM_MIX, GLA_WIDTH), BF16),
            pltpu.VMEM((TM_MIX, ATT_WIDTH + GLA_WIDTH), BF16),
            pltpu.VMEM((TM_MIX // GLA_CHUNK, GLA_QK_WIDTH), F32),
            pltpu.VMEM((GLA_HEADS, GLA_V_DIM, GLA_K_DIM), F32),
        ],
        compiler_params=params(("parallel", "arbitrary")),
        name="mixer",
    )(sinks.astype(F32), h, pos.reshape(n, 1), norm_mix.reshape(1, D_MODEL), w_r, wup,
      b_gate.reshape(1, GLA_QK_WIDTH), rope_rows, ltri, gla_norm.reshape(1, GLA_V_DIM))

    row = lambda i: (i, 0)
    return pl.pallas_call(
        _out_kernel,
        grid=(n // TM_OUT,),
        in_specs=[
            pl.BlockSpec((TM_OUT, D_MODEL), row),
            pl.BlockSpec((TM_OUT, ATT_WIDTH + GLA_WIDTH), row),
            pl.BlockSpec((TM_OUT, D_PLE), row),
            _const_spec((ATT_WIDTH + GLA_WIDTH, D_MODEL)),
            _const_spec((1, D_MODEL)),
            _const_spec((D_MODEL, D_MODEL)),
            _const_spec((D_PLE, D_MODEL)),
            _const_spec((1, D_MODEL)),
        ],
        out_specs=pl.BlockSpec((TM_OUT, D_MODEL), row),
        out_shape=jax.ShapeDtypeStruct((n, D_MODEL), F32),
        compiler_params=params(("parallel",)),
        name="outproj",
    )(h, y, p, w_out.astype(BF16), ple_norm.reshape(1, D_MODEL),
      w_ple_gate.astype(BF16), w_ple_proj.astype(BF16), out_norm.reshape(1, D_MODEL))


def kernel(x, p, positions, norm_mix, w_in, attn_sinks, w_gate_up, b_gate, gla_norm, w_out, ple_norm,
           w_ple_gate, w_ple_proj, final_norm):
    batch, seq, _ = x.shape
    depth = w_in.shape[0]
    assert depth == 1, "the final RMSNorm is fused into the single layer's output kernel"
    n = batch * seq
    out = _layer(x.reshape(n, D_MODEL), p[0].reshape(n, D_PLE), positions.reshape(n), norm_mix[0], w_in[0],
                 attn_sinks[0], w_gate_up[0], b_gate[0], gla_norm[0], w_out[0], ple_norm[0],
                 w_ple_gate[0], w_ple_proj[0], final_norm, batch, seq)
    return out.reshape(batch, seq, D_MODEL)
```

```python
import functools

import jax
import jax.numpy as jnp
import numpy as np
from jax import lax
from jax.experimental import pallas as pl
from jax.experimental.pallas import tpu as pltpu

D_MODEL = 1024
D_PLE = 256
ATT_HEADS = 16
ATT_KV_HEADS = 2
ATT_HEAD_DIM = 64
ATT_WIDTH = ATT_HEADS * ATT_HEAD_DIM
ATT_KV_WIDTH = ATT_KV_HEADS * ATT_HEAD_DIM
ATT_GROUP = ATT_HEADS // ATT_KV_HEADS
BLOCK = 128
ROPE_DIM = ATT_HEAD_DIM // 4
ROPE_HALF = ROPE_DIM // 2
ROPE_THETA = 500000.0
GLA_HEADS = 4
GLA_V_DIM = 256
GLA_K_DIM = 128
GLA_WIDTH = GLA_HEADS * GLA_V_DIM
GLA_QK_WIDTH = GLA_HEADS * GLA_K_DIM
GLA_GATE_RANK = 16
GLA_GATE_NORMALIZER = 16.0
GLA_CHUNK = 64
EPS = 1e-6

LANES = 128
SUBLANES = 8
VMEM_LIMIT = 56 * 1024 * 1024

C_Q = 0
C_KV = C_Q + ATT_WIDTH
C_GQ = C_KV + 4 * ATT_KV_WIDTH
C_GK = C_GQ + GLA_QK_WIDTH
C_GV = C_GK + GLA_QK_WIDTH
C_GL = C_GV + GLA_WIDTH
C_ZA = C_GL + LANES
C_ZG = C_ZA + ATT_WIDTH
C_END = C_ZG + GLA_WIDTH

TM_MIX = 512
TM_OUT = 512
CUM_BLK = 256

BF16 = jnp.bfloat16
F32 = jnp.float32


def _dot(a, b):
    return jnp.dot(a, b, preferred_element_type=F32)


def _dot_nt(a, b):
    return lax.dot_general(a, b, (((1,), (1,)), ((), ())), preferred_element_type=F32)


def _dot_tn(a, b):
    return lax.dot_general(a, b, (((0,), (0,)), ((), ())), preferred_element_type=F32)


def _rms(x, g):
    return x * lax.rsqrt(jnp.mean(x * x, axis=-1, keepdims=True) + EPS) * g


def _sigmoid(z):
    return 0.5 * (1.0 + jnp.tanh(0.5 * z))


def _inproj_pieces(x_ref, pos_ref, nmix_ref, w_ref, wup_ref, bg_ref, rope_ref, ltri_ref,
                   u_s, trig_s, glow_s, lg_s, b_s, q_s, kv_s, qd_s, ki_s, kt_s, gv_s, sg_s, bl_s):
    tm = x_ref.shape[0]

    def proj(c0, width):
        return _dot(u_s[...], w_ref[:, c0:c0 + width])

    def norm():
        u_s[...] = _rms(x_ref[...], nmix_ref[...]).astype(BF16)
        ang = pos_ref[...].astype(F32) * rope_ref[0:1, :]
        sin_t = jnp.sin(ang)
        trig_s[0] = jnp.cos(ang)
        trig_s[1] = sin_t * rope_ref[1:2, :]
        trig_s[2] = sin_t * rope_ref[2:3, :]

    def rope(z):
        up = pltpu.roll(z, LANES - ROPE_HALF, 1)
        dn = pltpu.roll(z, ROPE_HALF, 1)
        return z * trig_s[0] + up * trig_s[1] + dn * trig_s[2]

    def q_piece(lo, width):
        z = proj(C_Q + lo, width)
        for l in range(0, width, LANES):
            q_s[:, lo + l:lo + l + LANES] = rope(z[:, l:l + LANES]).astype(BF16)

    def kv_piece(lo, width):
        z = proj(C_KV + lo, width)
        for l in range(0, width, LANES):
            zl = z[:, l:l + LANES]
            kv_s[BLOCK:, lo + l:lo + l + LANES] = (rope(zl) if lo + l < 2 * ATT_KV_WIDTH else zl).astype(BF16)

    def gv_piece(lo, width):
        gv_s[:, lo:lo + width] = proj(C_GV + lo, width).astype(BF16)

    def sg_piece(lo, width):
        z = proj(C_ZA + lo, width)
        sg_s[:, lo:lo + width] = (z * _sigmoid(z)).astype(BF16)

    def gate_low():
        glow_s[...] = proj(C_GL, LANES).astype(BF16)

    def gate_logsig():
        logit = _dot(glow_s[...], wup_ref[...]) + bg_ref[...]
        lg = (jnp.minimum(logit, 0.0) - jnp.log(1.0 + jnp.exp(-jnp.abs(logit)))) * (1.0 / GLA_GATE_NORMALIZER)
        lg_hi = lg.astype(BF16)
        lg_s[0] = lg_hi
        lg_s[1] = (lg - lg_hi.astype(F32)).astype(BF16)

    def gate_cumsum():
        ltri = ltri_ref[...]
        for r in range(0, tm, CUM_BLK):
            b = _dot(ltri, lg_s[0, r:r + CUM_BLK, :]) + _dot(ltri, lg_s[1, r:r + CUM_BLK, :])
            nb = CUM_BLK // GLA_CHUNK
            b3 = b.reshape(nb, GLA_CHUNK, GLA_QK_WIDTH)
            b_last = b3[:, GLA_CHUNK - 1:GLA_CHUNK, :]
            b_s[0, r:r + CUM_BLK, :] = b
            b_s[1, r:r + CUM_BLK, :] = (b_last - b3).reshape(CUM_BLK, GLA_QK_WIDTH)
            bl_s[r // GLA_CHUNK:r // GLA_CHUNK + nb, :] = b_last.reshape(nb, GLA_QK_WIDTH)

    def gq_piece(lo, width):
        z = proj(C_GQ + lo, width) * (GLA_K_DIM ** -0.5)
        qd_s[:, lo:lo + width] = (z * jnp.exp(b_s[0, :, lo:lo + width])).astype(BF16)

    def gk_piece(lo, width):
        z = proj(C_GK + lo, width)
        ki_s[:, lo:lo + width] = (z * jnp.exp(-b_s[0, :, lo:lo + width])).astype(BF16)
        kt_s[:, lo:lo + width] = (z * jnp.exp(b_s[1, :, lo:lo + width])).astype(BF16)

    column_pieces = {"q": q_piece, "kv": kv_piece, "gv": gv_piece, "sg": sg_piece, "gq": gq_piece, "gk": gk_piece}
    single_pieces = {"norm": norm, "gate_low": gate_low, "gate_logsig": gate_logsig, "gate_cumsum": gate_cumsum}

    def run(spec):
        if spec[0] in single_pieces:
            single_pieces[spec[0]]()
        else:
            column_pieces[spec[0]](spec[1], spec[2])
    return run


_INPROJ_ROUNDS = (
    ((("q", 0, 512),), (("q", 512, 256),)),
    ((("q", 768, 256), ("kv", 0, 256)), (("kv", 256, 256),)),
    ((("gv", 0, 512),), (("gate_low",),)),
    ((("gv", 512, 512),), (("gate_logsig",),)),
    ((("sg", 0, 512),), (("gate_cumsum",),)),
    ((("sg", 512, 512),), (("gq", 0, 256),)),
    ((("sg", 1024, 512),), (("gq", 256, 256), ("gk", 0, 256))),
    ((("sg", 1536, 512),), (("gk", 256, 256),)),
)


def _attn_scores(j, g, q_s, kv_s, qstack_s, score_s):
    col = lax.broadcasted_iota(jnp.int32, (BLOCK, BLOCK), 1)
    lo_half = col < ATT_HEAD_DIM
    zero = jnp.zeros((), BF16)
    kcat = kv_s[j * BLOCK:(j + 2) * BLOCK, g * LANES:(g + 1) * LANES]
    for pr in range(ATT_GROUP // 2):
        c0 = (g * (ATT_GROUP // 2) + pr) * LANES
        qp = q_s[j * BLOCK:(j + 1) * BLOCK, c0:c0 + LANES]
        qstack_s[(2 * pr) * BLOCK:(2 * pr + 1) * BLOCK, :] = jnp.where(lo_half, qp, zero)
        qstack_s[(2 * pr + 1) * BLOCK:(2 * pr + 2) * BLOCK, :] = jnp.where(lo_half, zero, qp)
    score_s[...] = _dot_nt(qstack_s[...], kcat)


def _attn_softmax(g, prev_bias, sinks_ref, score_s, p_s):
    row = lax.broadcasted_iota(jnp.int32, (BLOCK, BLOCK), 0)
    col = lax.broadcasted_iota(jnp.int32, (BLOCK, BLOCK), 1)
    causal = col <= row
    dens = []
    for h in range(ATT_GROUP):
        rows = slice(h * BLOCK, (h + 1) * BLOCK)
        s_prev = score_s[rows, :BLOCK]
        if prev_bias is not None:
            s_prev = s_prev + prev_bias
        sf = jnp.where(causal, score_s[rows, BLOCK:], s_prev)
        sink = sinks_ref[g * ATT_GROUP + h]
        m = jnp.maximum(jnp.max(sf, axis=-1, keepdims=True), sink)
        e = jnp.exp(sf - m)
        dens.append(jnp.sum(e, axis=-1, keepdims=True) + jnp.exp(sink - m))
        p_s[rows, :BLOCK] = jnp.where(causal, 0.0, e).astype(BF16)
        p_s[rows, BLOCK:] = jnp.where(causal, e, 0.0).astype(BF16)
    return dens


def _attn_values(j, g, dens, p_s, kv_s, sg_s, y_ref):
    col = lax.broadcasted_iota(jnp.int32, (BLOCK, BLOCK), 1)
    lo_half = col < ATT_HEAD_DIM
    rows = slice(j * BLOCK, (j + 1) * BLOCK)
    vcat = kv_s[j * BLOCK:(j + 2) * BLOCK, (2 + g) * LANES:(3 + g) * LANES]
    o = _dot(p_s[...], vcat)
    for pr in range(ATT_GROUP // 2):
        he, ho = 2 * pr, 2 * pr + 1
        oe = o[he * BLOCK:(he + 1) * BLOCK] * (1.0 / dens[he])
        oo = o[ho * BLOCK:(ho + 1) * BLOCK] * (1.0 / dens[ho])
        c0 = (g * (ATT_GROUP // 2) + pr) * LANES
        y = jnp.where(lo_half, oe, oo) * sg_s[rows, c0:c0 + LANES].astype(F32)
        y_ref[rows, c0:c0 + LANES] = y.astype(BF16)


def _gla_independent(c, qd_s, ki_s, kt_s, gv_s):
    sl = slice(c * GLA_CHUNK, (c + 1) * GLA_CHUNK)
    out = []
    for h in range(GLA_HEADS):
        kc = slice(h * GLA_K_DIM, (h + 1) * GLA_K_DIM)
        vc = slice(h * GLA_V_DIM, (h + 1) * GLA_V_DIM)
        out.append((_dot_nt(qd_s[sl, kc], ki_s[sl, kc]), _dot_tn(gv_s[sl, vc], kt_s[sl, kc])))
    return out


def _gla_dependent(c, indep, qd_s, gv_s, sg_s, bl_s, gn_ref, st_s, y_ref):
    row = lax.broadcasted_iota(jnp.int32, (GLA_CHUNK, GLA_CHUNK), 0)
    col = lax.broadcasted_iota(jnp.int32, (GLA_CHUNK, GLA_CHUNK), 1)
    tril = col <= row
    gn = gn_ref[...]
    sl = slice(c * GLA_CHUNK, (c + 1) * GLA_CHUNK)
    for h in range(GLA_HEADS):
        kc = slice(h * GLA_K_DIM, (h + 1) * GLA_K_DIM)
        vc = slice(h * GLA_V_DIM, (h + 1) * GLA_V_DIM)
        oc = slice(ATT_WIDTH + h * GLA_V_DIM, ATT_WIDTH + (h + 1) * GLA_V_DIM)
        scores, kv_inc = indep[h]
        st = st_s[h]
        a = jnp.where(tril, scores, 0.0).astype(BF16)
        o = _dot(a, gv_s[sl, vc]) + _dot_nt(qd_s[sl, kc], st.astype(BF16))
        st_s[h] = st * jnp.exp(bl_s[c:c + 1, kc]) + kv_inc
        y = _rms(o, gn) * sg_s[sl, oc].astype(F32)
        y_ref[sl, oc] = y.astype(BF16)


def _mixer_kernel(sinks_ref, x_ref, pos_ref, nmix_ref, w_ref, wup_ref, bg_ref, rope_ref, ltri_ref, gn_ref,
                  y_ref, u_s, trig_s, glow_s, lg_s, b_s, qstack_s, score_s, p_s,
                  q_s, kv_s, qd_s, ki_s, kt_s, gv_s, sg_s, bl_s, st_s, *,
                  tiles_per_seq):
    i = pl.program_id(0)
    tm = x_ref.shape[0]
    wr = lax.rem(i, 2)
    rd = 1 - wr
    first = lax.rem(i - 1, tiles_per_seq) == 0

    @pl.when(i == 0)
    def _():
        for s in (q_s, kv_s, qd_s, ki_s, kt_s, gv_s, sg_s, bl_s):
            s[1] = jnp.zeros(s.shape[1:], s.dtype)

    @pl.when(first | (i == 0))
    def _():
        st_s[...] = jnp.zeros_like(st_s)

    inproj = _inproj_pieces(
        x_ref, pos_ref, nmix_ref, w_ref, wup_ref, bg_ref, rope_ref, ltri_ref, u_s, trig_s, glow_s, lg_s, b_s,
        q_s.at[wr], kv_s.at[wr], qd_s.at[wr], ki_s.at[wr], kt_s.at[wr], gv_s.at[wr], sg_s.at[wr], bl_s.at[wr])
    q_r, kv_r, sg_r, gv_r, qd_r = q_s.at[rd], kv_s.at[rd], sg_s.at[rd], gv_s.at[rd], qd_s.at[rd]
    first_bias = jnp.where(first, -jnp.inf, 0.0).astype(F32)

    assert len(_INPROJ_ROUNDS) == (tm // BLOCK) * ATT_KV_HEADS == tm // GLA_CHUNK
    inproj(("norm",))
    for r, (long_group, short_group) in enumerate(_INPROJ_ROUNDS):
        j, g = divmod(r, ATT_KV_HEADS)
        _attn_scores(j, g, q_r, kv_r, qstack_s, score_s)
        dens = _attn_softmax(g, first_bias if j == 0 else None, sinks_ref, score_s, p_s)
        gla_indep = _gla_independent(r, qd_r, ki_s.at[rd], kt_s.at[rd], gv_r)
        for spec in long_group:
            inproj(spec)
        _attn_values(j, g, dens, p_s, kv_r, sg_r, y_ref)
        _gla_dependent(r, gla_indep, qd_r, gv_r, sg_r, bl_s.at[rd], gn_ref, st_s, y_ref)
        for spec in short_group:
            inproj(spec)
    kv_s[wr, 0:BLOCK, :] = kv_s[rd, tm:tm + BLOCK, :]


def _out_kernel(x_ref, y_ref, p_ref, wo_ref, pn_ref, wpg_ref, wpp_ref, fn_ref, o_ref):
    h = x_ref[...] + _dot(y_ref[...], wo_ref[...])
    gate = _sigmoid(_dot(_rms(h, pn_ref[...]).astype(BF16), wpg_ref[...]))
    h = h + gate * _dot(p_ref[...].astype(BF16), wpp_ref[...])
    o_ref[...] = _rms(h, fn_ref[...])


def _const_spec(shape):
    nd = len(shape)
    return pl.BlockSpec(shape, lambda *_: (0,) * nd, pipeline_mode=pl.Buffered(1))


def _rope_rows():
    lane = np.arange(LANES) % ATT_HEAD_DIM
    inv_freq = ROPE_THETA ** (-np.arange(0, ROPE_DIM, 2, dtype=np.float32) / ROPE_DIM)
    rows = np.zeros((SUBLANES, LANES), np.float32)
    rows[0] = np.where(lane < ROPE_DIM, inv_freq[lane % ROPE_HALF], 0.0)
    rows[1] = np.where(lane < ROPE_HALF, -1.0, 0.0)
    rows[2] = np.where((lane >= ROPE_HALF) & (lane < ROPE_DIM), 1.0, 0.0)
    return rows


def _chunk_tril(n):
    i = np.arange(n)
    same = (i[:, None] // GLA_CHUNK) == (i[None, :] // GLA_CHUNK)
    return (same & (i[None, :] <= i[:, None])).astype(np.float32)


def _layer(h, p, pos, norm_mix, w_in, sinks, w_gate_up, b_gate, gla_norm, w_out, ple_norm,
           w_ple_gate, w_ple_proj, out_norm, batch, seq):
    n = batch * seq
    o = np.cumsum((0, ATT_WIDTH, ATT_KV_WIDTH, ATT_KV_WIDTH, GLA_QK_WIDTH, GLA_QK_WIDTH, GLA_WIDTH,
                   GLA_GATE_RANK, ATT_WIDTH, GLA_WIDTH))
    hd = ATT_HEAD_DIM
    k0, k1 = w_in[:, o[1]:o[1] + hd], w_in[:, o[1] + hd:o[2]]
    v0, v1 = w_in[:, o[2]:o[2] + hd], w_in[:, o[2] + hd:o[3]]
    w_r = jnp.concatenate(
        [w_in[:, o[0]:o[1]] * (ATT_HEAD_DIM ** -0.5), k0, k0, k1, k1, v0, v0, v1, v1,
         w_in[:, o[3]:o[6]],
         w_in[:, o[6]:o[7]], jnp.zeros((D_MODEL, LANES - GLA_GATE_RANK), w_in.dtype),
         w_in[:, o[7]:o[9]]], axis=1).astype(BF16)
    wup = jnp.concatenate([w_gate_up, jnp.zeros((LANES - GLA_GATE_RANK, GLA_QK_WIDTH), w_gate_up.dtype)],
                          axis=0).astype(BF16)
    rope_rows = jnp.asarray(_rope_rows())
    ltri = jnp.asarray(_chunk_tril(CUM_BLK), dtype=BF16)

    params = lambda sem: pltpu.CompilerParams(dimension_semantics=sem, vmem_limit_bytes=VMEM_LIMIT)

    nt = n // TM_MIX
    in_tile = lambda i: (jnp.minimum(i, nt - 1), 0)
    out_tile = lambda i: (jnp.maximum(i - 1, 0), 0)
    two = lambda *shape: (2,) + shape
    y = pl.pallas_call(
        functools.partial(_mixer_kernel, tiles_per_seq=seq // TM_MIX),
        grid=(nt + 1,),
        in_specs=[
            pl.BlockSpec(memory_space=pltpu.SMEM),
            pl.BlockSpec((TM_MIX, D_MODEL), in_tile),
            pl.BlockSpec((TM_MIX, 1), in_tile),
            _const_spec((1, D_MODEL)),
            _const_spec((D_MODEL, C_END)),
            _const_spec((LANES, GLA_QK_WIDTH)),
            _const_spec((1, GLA_QK_WIDTH)),
            _const_spec((SUBLANES, LANES)),
            _const_spec((CUM_BLK, CUM_BLK)),
            _const_spec((1, GLA_V_DIM)),
        ],
        out_specs=pl.BlockSpec((TM_MIX, ATT_WIDTH + GLA_WIDTH), out_tile),
        out_shape=jax.ShapeDtypeStruct((n, ATT_WIDTH + GLA_WIDTH), BF16),
        scratch_shapes=[
            pltpu.VMEM((TM_MIX, D_MODEL), BF16),
            pltpu.VMEM((3, TM_MIX, LANES), F32),
            pltpu.VMEM((TM_MIX, LANES), BF16),
            pltpu.VMEM((2, TM_MIX, GLA_QK_WIDTH), BF16),
            pltpu.VMEM((2, TM_MIX, GLA_QK_WIDTH), F32),
            pltpu.VMEM((ATT_GROUP * BLOCK, LANES), BF16),
            pltpu.VMEM((ATT_GROUP * BLOCK, 2 * BLOCK), F32),
            pltpu.VMEM((ATT_GROUP * BLOCK, 2 * BLOCK), BF16),
            pltpu.VMEM(two(TM_MIX, ATT_WIDTH), BF16),
            pltpu.VMEM(two(TM_MIX + BLOCK, 4 * ATT_KV_WIDTH), BF16),
            pltpu.VMEM(two(TM_MIX, GLA_QK_WIDTH), BF16),
            pltpu.VMEM(two(TM_MIX, GLA_QK_WIDTH), BF16),
            pltpu.VMEM(two(TM_MIX, GLA_QK_WIDTH), BF16),
            pltpu.VMEM(two(TM_MIX, GLA_WIDTH), BF16),
            pltpu.VMEM(two(TM_MIX, ATT_WIDTH + GLA_WIDTH), BF16),
            pltpu.VMEM(two(TM_MIX // GLA_CHUNK, GLA_QK_WIDTH), F32),
            pltpu.VMEM((GLA_HEADS, GLA_V_DIM, GLA_K_DIM), F32),
        ],
        compiler_params=params(("arbitrary",)),
        name="mixer",
    )(sinks.astype(F32), h, pos.reshape(n, 1), norm_mix.reshape(1, D_MODEL), w_r, wup,
      b_gate.reshape(1, GLA_QK_WIDTH), rope_rows, ltri, gla_norm.reshape(1, GLA_V_DIM))

    row = lambda i: (i, 0)
    return pl.pallas_call(
        _out_kernel,
        grid=(n // TM_OUT,),
        in_specs=[
            pl.BlockSpec((TM_OUT, D_MODEL), row),
            pl.BlockSpec((TM_OUT, ATT_WIDTH + GLA_WIDTH), row),
            pl.BlockSpec((TM_OUT, D_PLE), row),
            _const_spec((ATT_WIDTH + GLA_WIDTH, D_MODEL)),
            _const_spec((1, D_MODEL)),
            _const_spec((D_MODEL, D_MODEL)),
            _const_spec((D_PLE, D_MODEL)),
            _const_spec((1, D_MODEL)),
        ],
        out_specs=pl.BlockSpec((TM_OUT, D_MODEL), row),
        out_shape=jax.ShapeDtypeStruct((n, D_MODEL), F32),
        compiler_params=params(("parallel",)),
        name="outproj",
    )(h, y, p, w_out.astype(BF16), ple_norm.reshape(1, D_MODEL),
      w_ple_gate.astype(BF16), w_ple_proj.astype(BF16), out_norm.reshape(1, D_MODEL))


def kernel(x, p, positions, norm_mix, w_in, attn_sinks, w_gate_up, b_gate, gla_norm, w_out, ple_norm,
           w_ple_gate, w_ple_proj, final_norm):
    batch, seq, _ = x.shape
    depth = w_in.shape[0]
    assert depth == 1, "the final RMSNorm is fused into the single layer's output kernel"
    n = batch * seq
    out = _layer(x.reshape(n, D_MODEL), p[0].reshape(n, D_PLE), positions.reshape(n), norm_mix[0], w_in[0],
                 attn_sinks[0], w_gate_up[0], b_gate[0], gla_norm[0], w_out[0], ple_norm[0],
                 w_ple_gate[0], w_ple_proj[0], final_norm, batch, seq)
    return out.reshape(batch, seq, D_MODEL)
```

```python
import functools

import jax
import jax.numpy as jnp
import numpy as np
from jax import lax
from jax.experimental import pallas as pl
from jax.experimental.pallas import tpu as pltpu

D_MODEL = 1024
D_PLE = 256
ATT_HEADS = 16
ATT_KV_HEADS = 2
ATT_HEAD_DIM = 64
ATT_WIDTH = ATT_HEADS * ATT_HEAD_DIM
ATT_KV_WIDTH = ATT_KV_HEADS * ATT_HEAD_DIM
ATT_GROUP = ATT_HEADS // ATT_KV_HEADS
BLOCK = 128
ROPE_DIM = ATT_HEAD_DIM // 4
ROPE_HALF = ROPE_DIM // 2
ROPE_THETA = 500000.0
GLA_HEADS = 4
GLA_V_DIM = 256
GLA_K_DIM = 128
GLA_WIDTH = GLA_HEADS * GLA_V_DIM
GLA_QK_WIDTH = GLA_HEADS * GLA_K_DIM
GLA_GATE_RANK = 16
GLA_GATE_NORMALIZER = 16.0
GLA_CHUNK = 64
EPS = 1e-6

LANES = 128
SUBLANES = 8
VMEM_LIMIT = 56 * 1024 * 1024

C_Q = 0
C_KV = C_Q + ATT_WIDTH
C_GQ = C_KV + 2 * ATT_KV_WIDTH
C_GK = C_GQ + GLA_QK_WIDTH
C_GV = C_GK + GLA_QK_WIDTH
C_GL = C_GV + GLA_WIDTH
C_ZA = C_GL + LANES
C_ZG = C_ZA + ATT_WIDTH
C_END = C_ZG + GLA_WIDTH

TM_MIX = 512
TM_OUT = 512
CUM_BLK = 256

BF16 = jnp.bfloat16
F32 = jnp.float32


def _dot(a, b):
    return jnp.dot(a, b, preferred_element_type=F32)


def _dot_nt(a, b):
    return lax.dot_general(a, b, (((1,), (1,)), ((), ())), preferred_element_type=F32)


def _dot_tn(a, b):
    return lax.dot_general(a, b, (((0,), (0,)), ((), ())), preferred_element_type=F32)


def _rms(x, g):
    return x * lax.rsqrt(jnp.mean(x * x, axis=-1, keepdims=True) + EPS) * g


def _sigmoid(z):
    return 0.5 * (1.0 + jnp.tanh(0.5 * z))


def _inproj_pieces(x_ref, pos_ref, nmix_ref, w_ref, wup_ref, bg_ref, rope_ref, ltri_ref,
                   u_s, trig_s, glow_s, lg_s, b_s, q_s, kv_s, qd_s, ki_s, kt_s, gv_s, sg_s, bl_s):
    tm = x_ref.shape[0]

    def proj(c0, width):
        return _dot(u_s[...], w_ref[:, c0:c0 + width])

    def norm():
        u_s[...] = _rms(x_ref[...], nmix_ref[...]).astype(BF16)
        pos = pos_ref[0].astype(F32)
        ones = jnp.ones((ATT_HEAD_DIM - ROPE_DIM, LANES), F32)
        zeros = jnp.zeros((ATT_HEAD_DIM - ROPE_DIM, LANES), F32)
        zero8 = jnp.zeros((ROPE_HALF, LANES), F32)
        for r in range(tm // LANES):
            ang = rope_ref[...] * pos[r:r + 1, :]
            c8 = jnp.cos(ang)
            s8 = jnp.sin(ang)
            rows = slice(r * LANES, (r + 1) * LANES)
            trig_s[0, rows, :] = jnp.concatenate([c8, c8, ones] * 2, axis=0).T
            trig_s[1, rows, :] = jnp.concatenate([-s8, zero8, zeros] * 2, axis=0).T
            trig_s[2, rows, :] = jnp.concatenate([zero8, s8, zeros] * 2, axis=0).T

    def rope(z):
        up = pltpu.roll(z, LANES - ROPE_HALF, 1)
        dn = pltpu.roll(z, ROPE_HALF, 1)
        return z * trig_s[0] + up * trig_s[1] + dn * trig_s[2]

    def q_piece(lo, width):
        z = proj(C_Q + lo, width)
        for l in range(0, width, LANES):
            q_s[:, lo + l:lo + l + LANES] = rope(z[:, l:l + LANES]).astype(BF16)

    def kv_piece(lo, width):
        assert (lo, width) == (0, 2 * ATT_KV_WIDTH)
        z = proj(C_KV, width)
        lo_half = lax.broadcasted_iota(jnp.int32, (tm, LANES), 1) < ATT_HEAD_DIM
        k = rope(z[:, :LANES])
        k_sw = pltpu.roll(k, ATT_HEAD_DIM, 1)
        v = z[:, LANES:]
        v_sw = pltpu.roll(v, ATT_HEAD_DIM, 1)
        kv_s[BLOCK:, 0 * LANES:1 * LANES] = jnp.where(lo_half, k, k_sw).astype(BF16)
        kv_s[BLOCK:, 1 * LANES:2 * LANES] = jnp.where(lo_half, k_sw, k).astype(BF16)
        kv_s[BLOCK:, 2 * LANES:3 * LANES] = jnp.where(lo_half, v, 1.0).astype(BF16)
        kv_s[BLOCK:, 3 * LANES:4 * LANES] = jnp.where(lo_half, v_sw, 1.0).astype(BF16)

    def gv_piece(lo, width):
        gv_s[:, lo:lo + width] = proj(C_GV + lo, width).astype(BF16)

    def sg_piece(lo, width):
        z = proj(C_ZA + lo, width)
        sg_s[:, lo:lo + width] = (z * _sigmoid(z)).astype(BF16)

    def gate_low():
        glow_s[...] = proj(C_GL, LANES).astype(BF16)

    def gate_logsig():
        logit = _dot(glow_s[...], wup_ref[...]) + bg_ref[...]
        lg = (jnp.minimum(logit, 0.0) - jnp.log(1.0 + jnp.exp(-jnp.abs(logit)))) * (1.0 / GLA_GATE_NORMALIZER)
        lg_hi = lg.astype(BF16)
        lg_s[0] = lg_hi
        lg_s[1] = (lg - lg_hi.astype(F32)).astype(BF16)

    def gate_cumsum():
        ltri = ltri_ref[...]
        for r in range(0, tm, CUM_BLK):
            b = _dot(ltri, lg_s[0, r:r + CUM_BLK, :]) + _dot(ltri, lg_s[1, r:r + CUM_BLK, :])
            nb = CUM_BLK // GLA_CHUNK
            b3 = b.reshape(nb, GLA_CHUNK, GLA_QK_WIDTH)
            b_last = b3[:, GLA_CHUNK - 1:GLA_CHUNK, :]
            b_s[0, r:r + CUM_BLK, :] = b
            b_s[1, r:r + CUM_BLK, :] = (b_last - b3).reshape(CUM_BLK, GLA_QK_WIDTH)
            bl_s[r // GLA_CHUNK:r // GLA_CHUNK + nb, :] = b_last.reshape(nb, GLA_QK_WIDTH)

    def gq_piece(lo, width):
        z = proj(C_GQ + lo, width) * (GLA_K_DIM ** -0.5)
        qd_s[:, lo:lo + width] = (z * jnp.exp(b_s[0, :, lo:lo + width])).astype(BF16)

    def gk_piece(lo, width):
        z = proj(C_GK + lo, width)
        ki_s[:, lo:lo + width] = (z * jnp.exp(-b_s[0, :, lo:lo + width])).astype(BF16)
        kt_s[:, lo:lo + width] = (z * jnp.exp(b_s[1, :, lo:lo + width])).astype(BF16)

    column_pieces = {"q": q_piece, "kv": kv_piece, "gv": gv_piece, "sg": sg_piece, "gq": gq_piece, "gk": gk_piece}
    single_pieces = {"norm": norm, "gate_low": gate_low, "gate_logsig": gate_logsig, "gate_cumsum": gate_cumsum}

    def run(spec):
        if spec[0] in single_pieces:
            single_pieces[spec[0]]()
        else:
            column_pieces[spec[0]](spec[1], spec[2])
    return run


_INPROJ_ROUNDS = (
    ((("q", 0, 512),), (("q", 512, 256),)),
    ((("q", 768, 256), ("kv", 0, 256)), (("gate_low",),)),
    ((("gv", 0, 512),), (("gate_logsig",),)),
    ((("gv", 512, 512),), (("gate_cumsum",),)),
    ((("sg", 0, 512),), (("gq", 0, 256),)),
    ((("sg", 512, 512),), (("gq", 256, 256),)),
    ((("sg", 1024, 512),), (("gk", 0, 256),)),
    ((("sg", 1536, 512),), (("gk", 256, 256),)),
)


def _attn_scores(j, g, q_s, kv_s, qstack_s, score_s):
    col = lax.broadcasted_iota(jnp.int32, (BLOCK, BLOCK), 1)
    lo_half = col < ATT_HEAD_DIM
    zero = jnp.zeros((), BF16)
    kcat = kv_s[j * BLOCK:(j + 2) * BLOCK, g * LANES:(g + 1) * LANES]
    for pr in range(ATT_GROUP // 2):
        c0 = (g * (ATT_GROUP // 2) + pr) * LANES
        qp = q_s[j * BLOCK:(j + 1) * BLOCK, c0:c0 + LANES]
        qstack_s[(2 * pr) * BLOCK:(2 * pr + 1) * BLOCK, :] = jnp.where(lo_half, qp, zero)
        qstack_s[(2 * pr + 1) * BLOCK:(2 * pr + 2) * BLOCK, :] = jnp.where(lo_half, zero, qp)
    score_s[...] = _dot_nt(qstack_s[...], kcat)


def _attn_softmax(g, prev_bias, sinks_ref, score_s, p_s):
    row = lax.broadcasted_iota(jnp.int32, (BLOCK, BLOCK), 0)
    col = lax.broadcasted_iota(jnp.int32, (BLOCK, BLOCK), 1)
    causal = col <= row
    sink_terms = []
    for h in range(ATT_GROUP):
        rows = slice(h * BLOCK, (h + 1) * BLOCK)
        s_prev = score_s[rows, :BLOCK]
        if prev_bias is not None:
            s_prev = s_prev + prev_bias
        sf = jnp.where(causal, score_s[rows, BLOCK:], s_prev)
        sink = sinks_ref[g * ATT_GROUP + h]
        m = jnp.maximum(jnp.max(sf, axis=-1, keepdims=True), sink)
        e = jnp.exp(sf - m)
        sink_terms.append(jnp.exp(sink - m))
        p_s[rows, :BLOCK] = jnp.where(causal, 0.0, e).astype(BF16)
        p_s[rows, BLOCK:] = jnp.where(causal, e, 0.0).astype(BF16)
    return sink_terms


def _attn_values(j, g, sink_terms, p_s, kv_s, sg_s, y_ref):
    col = lax.broadcasted_iota(jnp.int32, (BLOCK, BLOCK), 1)
    lo_half = col < ATT_HEAD_DIM
    rows = slice(j * BLOCK, (j + 1) * BLOCK)
    vcat = kv_s[j * BLOCK:(j + 2) * BLOCK, (2 + g) * LANES:(3 + g) * LANES]
    o = _dot(p_s[...], vcat)
    for pr in range(ATT_GROUP // 2):
        he, ho = 2 * pr, 2 * pr + 1
        oe = o[he * BLOCK:(he + 1) * BLOCK]
        oo = o[ho * BLOCK:(ho + 1) * BLOCK]
        num = jnp.where(lo_half, oe, pltpu.roll(oo, ATT_HEAD_DIM, 1))
        den = jnp.where(lo_half, pltpu.roll(oe, ATT_HEAD_DIM, 1), oo) \
            + jnp.where(lo_half, sink_terms[he], sink_terms[ho])
        c0 = (g * (ATT_GROUP // 2) + pr) * LANES
        y = num * (1.0 / den) * sg_s[rows, c0:c0 + LANES].astype(F32)
        y_ref[rows, c0:c0 + LANES] = y.astype(BF16)


def _gla_independent(c, qd_s, ki_s, kt_s, gv_s):
    sl = slice(c * GLA_CHUNK, (c + 1) * GLA_CHUNK)
    out = []
    for h in range(GLA_HEADS):
        kc = slice(h * GLA_K_DIM, (h + 1) * GLA_K_DIM)
        vc = slice(h * GLA_V_DIM, (h + 1) * GLA_V_DIM)
        out.append((_dot_nt(qd_s[sl, kc], ki_s[sl, kc]), _dot_tn(gv_s[sl, vc], kt_s[sl, kc])))
    return out


def _gla_dependent(c, indep, qd_s, gv_s, sg_s, bl_s, gn_ref, st_s, y_ref):
    row = lax.broadcasted_iota(jnp.int32, (GLA_CHUNK, GLA_CHUNK), 0)
    col = lax.broadcasted_iota(jnp.int32, (GLA_CHUNK, GLA_CHUNK), 1)
    tril = col <= row
    gn = gn_ref[...]
    sl = slice(c * GLA_CHUNK, (c + 1) * GLA_CHUNK)
    for h in range(GLA_HEADS):
        kc = slice(h * GLA_K_DIM, (h + 1) * GLA_K_DIM)
        vc = slice(h * GLA_V_DIM, (h + 1) * GLA_V_DIM)
        oc = slice(ATT_WIDTH + h * GLA_V_DIM, ATT_WIDTH + (h + 1) * GLA_V_DIM)
        scores, kv_inc = indep[h]
        st = st_s[h]
        a = jnp.where(tril, scores, 0.0).astype(BF16)
        o = _dot(a, gv_s[sl, vc]) + _dot_nt(qd_s[sl, kc], st.astype(BF16))
        st_s[h] = st * jnp.exp(bl_s[c:c + 1, kc]) + kv_inc
        y = _rms(o, gn) * sg_s[sl, oc].astype(F32)
        y_ref[sl, oc] = y.astype(BF16)


def _mixer_kernel(sinks_ref, x_ref, pos_ref, nmix_ref, w_ref, wup_ref, bg_ref, rope_ref, ltri_ref, gn_ref,
                  y_ref, u_s, trig_s, glow_s, lg_s, b_s, qstack_s, score_s, p_s,
                  q_s, kv_s, qd_s, ki_s, kt_s, gv_s, sg_s, bl_s, st_s, *,
                  tiles_per_seq):
    i = pl.program_id(0)
    tm = x_ref.shape[0]
    first = lax.rem(i - 1, tiles_per_seq) == 0

    @pl.when(i == 0)
    def _():
        for s in (q_s, kv_s, qd_s, ki_s, kt_s, gv_s, sg_s, bl_s):
            s[1] = jnp.zeros(s.shape[1:], s.dtype)

    @pl.when(first | (i == 0))
    def _():
        st_s[...] = jnp.zeros_like(st_s)

    def step(wr):
        rd = 1 - wr
        inproj = _inproj_pieces(
            x_ref, pos_ref, nmix_ref, w_ref, wup_ref, bg_ref, rope_ref, ltri_ref, u_s, trig_s, glow_s, lg_s, b_s,
            q_s.at[wr], kv_s.at[wr], qd_s.at[wr], ki_s.at[wr], kt_s.at[wr], gv_s.at[wr], sg_s.at[wr],
            bl_s.at[wr])
        q_r, kv_r, sg_r, gv_r, qd_r = q_s.at[rd], kv_s.at[rd], sg_s.at[rd], gv_s.at[rd], qd_s.at[rd]
        first_bias = jnp.where(first, -jnp.inf, 0.0).astype(F32)

        inproj(("norm",))
        for r, (long_group, short_group) in enumerate(_INPROJ_ROUNDS):
            j, g = divmod(r, ATT_KV_HEADS)
            _attn_scores(j, g, q_r, kv_r, qstack_s, score_s)
            sink_terms = _attn_softmax(g, first_bias if j == 0 else None, sinks_ref, score_s, p_s)
            gla_indep = _gla_independent(r, qd_r, ki_s.at[rd], kt_s.at[rd], gv_r)
            for spec in long_group:
                inproj(spec)
            _attn_values(j, g, sink_terms, p_s, kv_r, sg_r, y_ref)
            _gla_dependent(r, gla_indep, qd_r, gv_r, sg_r, bl_s.at[rd], gn_ref, st_s, y_ref)
            for spec in short_group:
                inproj(spec)
        kv_s[wr, 0:BLOCK, :] = kv_s[rd, tm:tm + BLOCK, :]

    assert len(_INPROJ_ROUNDS) == (tm // BLOCK) * ATT_KV_HEADS == tm // GLA_CHUNK
    parity = lax.rem(i, 2)
    for wr in range(2):
        pl.when(parity == wr)(functools.partial(step, wr))


def _out_kernel(x_ref, y_ref, p_ref, wo_ref, pn_ref, wpg_ref, wpp_ref, fn_ref, o_ref):
    h = x_ref[...] + _dot(y_ref[...], wo_ref[...])
    gate = _sigmoid(_dot(_rms(h, pn_ref[...]).astype(BF16), wpg_ref[...]))
    h = h + gate * _dot(p_ref[...].astype(BF16), wpp_ref[...])
    o_ref[...] = _rms(h, fn_ref[...])


def _const_spec(shape):
    nd = len(shape)
    return pl.BlockSpec(shape, lambda *_: (0,) * nd, pipeline_mode=pl.Buffered(1))


def _rope_freqs():
    assert ROPE_HALF == SUBLANES
    inv_freq = ROPE_THETA ** (-jnp.arange(0, ROPE_DIM, 2, dtype=F32) / ROPE_DIM)
    return jnp.broadcast_to(inv_freq[:, None], (ROPE_HALF, LANES))


def _chunk_tril(n):
    i = np.arange(n)
    same = (i[:, None] // GLA_CHUNK) == (i[None, :] // GLA_CHUNK)
    return (same & (i[None, :] <= i[:, None])).astype(np.float32)


def _layer(h, p, pos, norm_mix, w_in, sinks, w_gate_up, b_gate, gla_norm, w_out, ple_norm,
           w_ple_gate, w_ple_proj, out_norm, batch, seq):
    n = batch * seq
    o = np.cumsum((0, ATT_WIDTH, ATT_KV_WIDTH, ATT_KV_WIDTH, GLA_QK_WIDTH, GLA_QK_WIDTH, GLA_WIDTH,
                   GLA_GATE_RANK, ATT_WIDTH, GLA_WIDTH))
    w_r = jnp.concatenate(
        [w_in[:, o[0]:o[1]] * (ATT_HEAD_DIM ** -0.5), w_in[:, o[1]:o[7]],
         jnp.zeros((D_MODEL, LANES - GLA_GATE_RANK), w_in.dtype),
         w_in[:, o[7]:o[9]]], axis=1).astype(BF16)
    wup = jnp.concatenate([w_gate_up, jnp.zeros((LANES - GLA_GATE_RANK, GLA_QK_WIDTH), w_gate_up.dtype)],
                          axis=0).astype(BF16)
    rope_freqs = _rope_freqs()
    ltri = jnp.asarray(_chunk_tril(CUM_BLK), dtype=BF16)

    params = lambda sem: pltpu.CompilerParams(dimension_semantics=sem, vmem_limit_bytes=VMEM_LIMIT)

    nt = n // TM_MIX
    in_tile = lambda i: (jnp.minimum(i, nt - 1), 0)
    out_tile = lambda i: (jnp.maximum(i - 1, 0), 0)
    two = lambda *shape: (2,) + shape
    y = pl.pallas_call(
        functools.partial(_mixer_kernel, tiles_per_seq=seq // TM_MIX),
        grid=(nt + 1,),
        in_specs=[
            pl.BlockSpec(memory_space=pltpu.SMEM),
            pl.BlockSpec((TM_MIX, D_MODEL), in_tile),
            pl.BlockSpec((1, TM_MIX // LANES, LANES), lambda i: (jnp.minimum(i, nt - 1), 0, 0)),
            _const_spec((1, D_MODEL)),
            _const_spec((D_MODEL, C_END)),
            _const_spec((LANES, GLA_QK_WIDTH)),
            _const_spec((1, GLA_QK_WIDTH)),
            _const_spec((SUBLANES, LANES)),
            _const_spec((CUM_BLK, CUM_BLK)),
            _const_spec((1, GLA_V_DIM)),
        ],
        out_specs=pl.BlockSpec((TM_MIX, ATT_WIDTH + GLA_WIDTH), out_tile),
        out_shape=jax.ShapeDtypeStruct((n, ATT_WIDTH + GLA_WIDTH), BF16),
        scratch_shapes=[
            pltpu.VMEM((TM_MIX, D_MODEL), BF16),
            pltpu.VMEM((3, TM_MIX, LANES), F32),
            pltpu.VMEM((TM_MIX, LANES), BF16),
            pltpu.VMEM((2, TM_MIX, GLA_QK_WIDTH), BF16),
            pltpu.VMEM((2, TM_MIX, GLA_QK_WIDTH), F32),
            pltpu.VMEM((ATT_GROUP * BLOCK, LANES), BF16),
            pltpu.VMEM((ATT_GROUP * BLOCK, 2 * BLOCK), F32),
            pltpu.VMEM((ATT_GROUP * BLOCK, 2 * BLOCK), BF16),
            pltpu.VMEM(two(TM_MIX, ATT_WIDTH), BF16),
            pltpu.VMEM(two(TM_MIX + BLOCK, 4 * ATT_KV_WIDTH), BF16),
            pltpu.VMEM(two(TM_MIX, GLA_QK_WIDTH), BF16),
            pltpu.VMEM(two(TM_MIX, GLA_QK_WIDTH), BF16),
            pltpu.VMEM(two(TM_MIX, GLA_QK_WIDTH), BF16),
            pltpu.VMEM(two(TM_MIX, GLA_WIDTH), BF16),
            pltpu.VMEM(two(TM_MIX, ATT_WIDTH + GLA_WIDTH), BF16),
            pltpu.VMEM(two(TM_MIX // GLA_CHUNK, GLA_QK_WIDTH), F32),
            pltpu.VMEM((GLA_HEADS, GLA_V_DIM, GLA_K_DIM), F32),
        ],
        compiler_params=params(("arbitrary",)),
        name="mixer",
    )(sinks.astype(F32), h, pos.reshape(nt, TM_MIX // LANES, LANES), norm_mix.reshape(1, D_MODEL), w_r, wup,
      b_gate.reshape(1, GLA_QK_WIDTH), rope_freqs, ltri, gla_norm.reshape(1, GLA_V_DIM))

    row = lambda i: (i, 0)
    return pl.pallas_call(
        _out_kernel,
        grid=(n // TM_OUT,),
        in_specs=[
            pl.BlockSpec((TM_OUT, D_MODEL), row),
            pl.BlockSpec((TM_OUT, ATT_WIDTH + GLA_WIDTH), row),
            pl.BlockSpec((TM_OUT, D_PLE), row),
            _const_spec((ATT_WIDTH + GLA_WIDTH, D_MODEL)),
            _const_spec((1, D_MODEL)),
            _const_spec((D_MODEL, D_MODEL)),
            _const_spec((D_PLE, D_MODEL)),
            _const_spec((1, D_MODEL)),
        ],
        out_specs=pl.BlockSpec((TM_OUT, D_MODEL), row),
        out_shape=jax.ShapeDtypeStruct((n, D_MODEL), F32),
        compiler_params=params(("parallel",)),
        name="outproj",
    )(h, y, p, w_out.astype(BF16), ple_norm.reshape(1, D_MODEL),
      w_ple_gate.astype(BF16), w_ple_proj.astype(BF16), out_norm.reshape(1, D_MODEL))


def kernel(x, p, positions, norm_mix, w_in, attn_sinks, w_gate_up, b_gate, gla_norm, w_out, ple_norm,
           w_ple_gate, w_ple_proj, final_norm):
    batch, seq, _ = x.shape
    depth = w_in.shape[0]
    assert depth == 1, "the final RMSNorm is fused into the single layer's output kernel"
    n = batch * seq
    out = _layer(x.reshape(n, D_MODEL), p[0].reshape(n, D_PLE), positions.reshape(n), norm_mix[0], w_in[0],
                 attn_sinks[0], w_gate_up[0], b_gate[0], gla_norm[0], w_out[0], ple_norm[0],
                 w_ple_gate[0], w_ple_proj[0], final_norm, batch, seq)
    return out.reshape(batch, seq, D_MODEL)
```

```python
import functools

import jax
import jax.numpy as jnp
import numpy as np
from jax import lax
from jax.experimental import pallas as pl
from jax.experimental.pallas import tpu as pltpu

D_MODEL = 1024
D_PLE = 256
ATT_HEADS = 16
ATT_KV_HEADS = 2
ATT_HEAD_DIM = 64
ATT_WIDTH = ATT_HEADS * ATT_HEAD_DIM
ATT_KV_WIDTH = ATT_KV_HEADS * ATT_HEAD_DIM
ATT_GROUP = ATT_HEADS // ATT_KV_HEADS
BLOCK = 128
ROPE_DIM = ATT_HEAD_DIM // 4
ROPE_HALF = ROPE_DIM // 2
ROPE_THETA = 500000.0
GLA_HEADS = 4
GLA_V_DIM = 256
GLA_K_DIM = 128
GLA_WIDTH = GLA_HEADS * GLA_V_DIM
GLA_QK_WIDTH = GLA_HEADS * GLA_K_DIM
GLA_GATE_RANK = 16
GLA_GATE_NORMALIZER = 16.0
GLA_CHUNK = 64
EPS = 1e-6

LANES = 128
SUBLANES = 8
VMEM_LIMIT = 56 * 1024 * 1024

C_Q = 0
C_KV = C_Q + ATT_WIDTH
C_GQ = C_KV + 2 * ATT_KV_WIDTH
C_GK = C_GQ + GLA_QK_WIDTH
C_GV = C_GK + GLA_QK_WIDTH
C_GL = C_GV + GLA_WIDTH
C_ZA = C_GL + LANES
C_ZG = C_ZA + ATT_WIDTH
C_END = C_ZG + GLA_WIDTH

TM_MIX = 512
TM_OUT = 512
CUM_BLK = 256

BF16 = jnp.bfloat16
F32 = jnp.float32


def _dot(a, b):
    return jnp.dot(a, b, preferred_element_type=F32)


def _dot_nt(a, b):
    return lax.dot_general(a, b, (((1,), (1,)), ((), ())), preferred_element_type=F32)


def _dot_tn(a, b):
    return lax.dot_general(a, b, (((0,), (0,)), ((), ())), preferred_element_type=F32)


def _rms(x, g):
    return x * lax.rsqrt(jnp.mean(x * x, axis=-1, keepdims=True) + EPS) * g


def _sigmoid(z):
    return 0.5 * (1.0 + jnp.tanh(0.5 * z))


def _inproj_pieces(x_ref, pos_ref, nmix_ref, w_ref, wt_ref, wup_ref, bg_ref, rope_ref, ltri_ref,
                   u_s, trig_s, glow_s, lg_s, b_s, q_s, kv_s, qd_s, ki_s, kt_s, gv_s, sg_s, bl_s):
    tm = x_ref.shape[0]

    def proj(c0, width):
        return _dot(u_s[...], w_ref[:, c0:c0 + width])

    def norm():
        u_s[...] = _rms(x_ref[...], nmix_ref[...]).astype(BF16)
        pos = pos_ref[0].astype(F32)
        ones = jnp.ones((ATT_HEAD_DIM - ROPE_DIM, LANES), F32)
        zeros = jnp.zeros((ATT_HEAD_DIM - ROPE_DIM, LANES), F32)
        zero8 = jnp.zeros((ROPE_HALF, LANES), F32)
        for r in range(tm // LANES):
            ang = rope_ref[...] * pos[r:r + 1, :]
            c8 = jnp.cos(ang)
            s8 = jnp.sin(ang)
            rows = slice(r * LANES, (r + 1) * LANES)
            trig_s[0, rows, :] = jnp.concatenate([c8, c8, ones] * 2, axis=0).T
            trig_s[1, rows, :] = jnp.concatenate([-s8, zero8, zeros] * 2, axis=0).T
            trig_s[2, rows, :] = jnp.concatenate([zero8, s8, zeros] * 2, axis=0).T

    def rope(z):
        up = pltpu.roll(z, LANES - ROPE_HALF, 1)
        dn = pltpu.roll(z, ROPE_HALF, 1)
        return z * trig_s[0] + up * trig_s[1] + dn * trig_s[2]

    def q_piece(lo, width):
        z = proj(C_Q + lo, width)
        for l in range(0, width, LANES):
            q_s[:, lo + l:lo + l + LANES] = rope(z[:, l:l + LANES]).astype(BF16)

    def kv_piece(lo, width):
        assert (lo, width) == (0, 2 * ATT_KV_WIDTH)
        z = proj(C_KV, width)
        lo_half = lax.broadcasted_iota(jnp.int32, (tm, LANES), 1) < ATT_HEAD_DIM
        k = rope(z[:, :LANES])
        k_sw = pltpu.roll(k, ATT_HEAD_DIM, 1)
        v = z[:, LANES:]
        v_sw = pltpu.roll(v, ATT_HEAD_DIM, 1)
        kv_s[BLOCK:, 0 * LANES:1 * LANES] = jnp.where(lo_half, k, k_sw).astype(BF16)
        kv_s[BLOCK:, 1 * LANES:2 * LANES] = jnp.where(lo_half, k_sw, k).astype(BF16)
        kv_s[BLOCK:, 2 * LANES:3 * LANES] = jnp.where(lo_half, v, 1.0).astype(BF16)
        kv_s[BLOCK:, 3 * LANES:4 * LANES] = jnp.where(lo_half, v_sw, 1.0).astype(BF16)

    def gv_piece(lo, width):
        gv_s[:, lo:lo + width] = proj(C_GV + lo, width).astype(BF16)

    def sg_piece(lo, width):
        z = _dot(u_s[...], wt_ref[:, lo:lo + width])
        sg_s[:, lo:lo + width] = (z * _sigmoid(z)).astype(BF16)

    def gate_low():
        glow_s[...] = proj(C_GL, LANES).astype(BF16)

    def gate_logsig():
        logit = _dot(glow_s[...], wup_ref[...]) + bg_ref[...]
        lg = (jnp.minimum(logit, 0.0) - jnp.log(1.0 + jnp.exp(-jnp.abs(logit)))) * (1.0 / GLA_GATE_NORMALIZER)
        lg_hi = lg.astype(BF16)
        lg_s[0] = lg_hi
        lg_s[1] = (lg - lg_hi.astype(F32)).astype(BF16)

    def gate_cumsum():
        ltri = ltri_ref[...]
        for r in range(0, tm, CUM_BLK):
            b = _dot(ltri, lg_s[0, r:r + CUM_BLK, :]) + _dot(ltri, lg_s[1, r:r + CUM_BLK, :])
            nb = CUM_BLK // GLA_CHUNK
            b3 = b.reshape(nb, GLA_CHUNK, GLA_QK_WIDTH)
            b_last = b3[:, GLA_CHUNK - 1:GLA_CHUNK, :]
            b_s[0, r:r + CUM_BLK, :] = b
            b_s[1, r:r + CUM_BLK, :] = (b_last - b3).reshape(CUM_BLK, GLA_QK_WIDTH)
            bl_s[r // GLA_CHUNK:r // GLA_CHUNK + nb, :] = b_last.reshape(nb, GLA_QK_WIDTH)

    def gq_piece(lo, width):
        z = proj(C_GQ + lo, width) * (GLA_K_DIM ** -0.5)
        qd_s[:, lo:lo + width] = (z * jnp.exp(b_s[0, :, lo:lo + width])).astype(BF16)

    def gk_piece(lo, width):
        z = proj(C_GK + lo, width)
        ki_s[:, lo:lo + width] = (z * jnp.exp(-b_s[0, :, lo:lo + width])).astype(BF16)
        kt_s[:, lo:lo + width] = (z * jnp.exp(b_s[1, :, lo:lo + width])).astype(BF16)

    column_pieces = {"q": q_piece, "kv": kv_piece, "gv": gv_piece, "sg": sg_piece, "gq": gq_piece, "gk": gk_piece}
    single_pieces = {"norm": norm, "gate_low": gate_low, "gate_logsig": gate_logsig, "gate_cumsum": gate_cumsum}

    def run(spec):
        if spec[0] in single_pieces:
            single_pieces[spec[0]]()
        else:
            column_pieces[spec[0]](spec[1], spec[2])
    return run


_INPROJ_ROUNDS = (
    ((("q", 0, 512),), (("q", 512, 256),)),
    ((("q", 768, 256), ("kv", 0, 256)), (("gate_low",),)),
    ((("gv", 0, 512),), (("gate_logsig",),)),
    ((("gv", 512, 512),), (("gate_cumsum",),)),
    ((("sg", 0, 512),), (("gq", 0, 256),)),
    ((("sg", 512, 512),), (("gq", 256, 256),)),
    ((("sg", 1024, 512),), (("gk", 0, 256),)),
    ((("sg", 1536, 512),), (("gk", 256, 256),)),
)


def _attn_scores(j, g, q_s, kv_s, qstack_s, score_s):
    col = lax.broadcasted_iota(jnp.int32, (BLOCK, BLOCK), 1)
    lo_half = col < ATT_HEAD_DIM
    zero = jnp.zeros((), BF16)
    kcat = kv_s[j * BLOCK:(j + 2) * BLOCK, g * LANES:(g + 1) * LANES]
    for pr in range(ATT_GROUP // 2):
        c0 = (g * (ATT_GROUP // 2) + pr) * LANES
        qp = q_s[j * BLOCK:(j + 1) * BLOCK, c0:c0 + LANES]
        qstack_s[(2 * pr) * BLOCK:(2 * pr + 1) * BLOCK, :] = jnp.where(lo_half, qp, zero)
        qstack_s[(2 * pr + 1) * BLOCK:(2 * pr + 2) * BLOCK, :] = jnp.where(lo_half, zero, qp)
    score_s[...] = _dot_nt(qstack_s[...], kcat)


def _attn_softmax(g, prev_bias, sinks_ref, score_s, p_s):
    row = lax.broadcasted_iota(jnp.int32, (BLOCK, BLOCK), 0)
    col = lax.broadcasted_iota(jnp.int32, (BLOCK, BLOCK), 1)
    causal = col <= row
    sink_terms = []
    for h in range(ATT_GROUP):
        rows = slice(h * BLOCK, (h + 1) * BLOCK)
        s_prev = score_s[rows, :BLOCK]
        if prev_bias is not None:
            s_prev = s_prev + prev_bias
        sf = jnp.where(causal, score_s[rows, BLOCK:], s_prev)
        sink = sinks_ref[g * ATT_GROUP + h]
        m = jnp.maximum(jnp.max(sf, axis=-1, keepdims=True), sink)
        e = jnp.exp(sf - m)
        sink_terms.append(jnp.exp(sink - m))
        p_s[rows, :BLOCK] = jnp.where(causal, 0.0, e).astype(BF16)
        p_s[rows, BLOCK:] = jnp.where(causal, e, 0.0).astype(BF16)
    return sink_terms


def _attn_values(j, g, sink_terms, p_s, kv_s, sg_s, y_ref):
    col = lax.broadcasted_iota(jnp.int32, (BLOCK, BLOCK), 1)
    lo_half = col < ATT_HEAD_DIM
    rows = slice(j * BLOCK, (j + 1) * BLOCK)
    vcat = kv_s[j * BLOCK:(j + 2) * BLOCK, (2 + g) * LANES:(3 + g) * LANES]
    o = _dot(p_s[...], vcat)
    for pr in range(ATT_GROUP // 2):
        he, ho = 2 * pr, 2 * pr + 1
        oe = o[he * BLOCK:(he + 1) * BLOCK]
        oo = o[ho * BLOCK:(ho + 1) * BLOCK]
        num = jnp.where(lo_half, oe, pltpu.roll(oo, ATT_HEAD_DIM, 1))
        den = jnp.where(lo_half, pltpu.roll(oe, ATT_HEAD_DIM, 1), oo) \
            + jnp.where(lo_half, sink_terms[he], sink_terms[ho])
        c0 = (g * (ATT_GROUP // 2) + pr) * LANES
        y = num * (1.0 / den) * sg_s[rows, c0:c0 + LANES].astype(F32)
        y_ref[rows, c0:c0 + LANES] = y.astype(BF16)


def _gla_independent(c, qd_s, ki_s, kt_s, gv_s):
    sl = slice(c * GLA_CHUNK, (c + 1) * GLA_CHUNK)
    out = []
    for h in range(GLA_HEADS):
        kc = slice(h * GLA_K_DIM, (h + 1) * GLA_K_DIM)
        vc = slice(h * GLA_V_DIM, (h + 1) * GLA_V_DIM)
        out.append((_dot_nt(qd_s[sl, kc], ki_s[sl, kc]), _dot_tn(gv_s[sl, vc], kt_s[sl, kc])))
    return out


def _gla_dependent(c, indep, qd_s, gv_s, sg_s, bl_s, gn_ref, st_s, y_ref):
    row = lax.broadcasted_iota(jnp.int32, (GLA_CHUNK, GLA_CHUNK), 0)
    col = lax.broadcasted_iota(jnp.int32, (GLA_CHUNK, GLA_CHUNK), 1)
    tril = col <= row
    gn = gn_ref[...]
    sl = slice(c * GLA_CHUNK, (c + 1) * GLA_CHUNK)
    for h in range(GLA_HEADS):
        kc = slice(h * GLA_K_DIM, (h + 1) * GLA_K_DIM)
        vc = slice(h * GLA_V_DIM, (h + 1) * GLA_V_DIM)
        oc = slice(ATT_WIDTH + h * GLA_V_DIM, ATT_WIDTH + (h + 1) * GLA_V_DIM)
        scores, kv_inc = indep[h]
        st = st_s[h]
        a = jnp.where(tril, scores, 0.0).astype(BF16)
        o = _dot(a, gv_s[sl, vc]) + _dot_nt(qd_s[sl, kc], st.astype(BF16))
        st_s[h] = st * jnp.exp(bl_s[c:c + 1, kc]) + kv_inc
        y = _rms(o, gn) * sg_s[sl, oc].astype(F32)
        y_ref[sl, oc] = y.astype(BF16)


def _mixer_kernel(sinks_ref, x_ref, pos_ref, nmix_ref, w_ref, wt_ref, wup_ref, bg_ref, rope_ref, ltri_ref, gn_ref,
                  y_ref, u_s, trig_s, glow_s, lg_s, b_s, qstack_s, score_s, p_s,
                  q_s, kv_s, qd_s, ki_s, kt_s, gv_s, sg_s, bl_s, st_s, *,
                  tiles_per_seq):
    i = pl.program_id(0)
    tm = x_ref.shape[0]
    first = lax.rem(i - 1, tiles_per_seq) == 0

    @pl.when(i == 0)
    def _():
        for s in (q_s, kv_s, qd_s, ki_s, kt_s, gv_s, sg_s, bl_s):
            s[1] = jnp.zeros(s.shape[1:], s.dtype)

    @pl.when(first | (i == 0))
    def _():
        st_s[...] = jnp.zeros_like(st_s)

    def step(wr):
        rd = 1 - wr
        inproj = _inproj_pieces(
            x_ref, pos_ref, nmix_ref, w_ref, wt_ref, wup_ref, bg_ref, rope_ref, ltri_ref, u_s, trig_s, glow_s, lg_s, b_s,
            q_s.at[wr], kv_s.at[wr], qd_s.at[wr], ki_s.at[wr], kt_s.at[wr], gv_s.at[wr], sg_s.at[wr],
            bl_s.at[wr])
        q_r, kv_r, sg_r, gv_r, qd_r = q_s.at[rd], kv_s.at[rd], sg_s.at[rd], gv_s.at[rd], qd_s.at[rd]
        first_bias = jnp.where(first, -jnp.inf, 0.0).astype(F32)

        inproj(("norm",))
        for r, (long_group, short_group) in enumerate(_INPROJ_ROUNDS):
            j, g = divmod(r, ATT_KV_HEADS)
            _attn_scores(j, g, q_r, kv_r, qstack_s, score_s)
            sink_terms = _attn_softmax(g, first_bias if j == 0 else None, sinks_ref, score_s, p_s)
            gla_indep = _gla_independent(r, qd_r, ki_s.at[rd], kt_s.at[rd], gv_r)
            for spec in long_group:
                inproj(spec)
            _attn_values(j, g, sink_terms, p_s, kv_r, sg_r, y_ref)
            _gla_dependent(r, gla_indep, qd_r, gv_r, sg_r, bl_s.at[rd], gn_ref, st_s, y_ref)
            for spec in short_group:
                inproj(spec)
        kv_s[wr, 0:BLOCK, :] = kv_s[rd, tm:tm + BLOCK, :]

    assert len(_INPROJ_ROUNDS) == (tm // BLOCK) * ATT_KV_HEADS == tm // GLA_CHUNK
    parity = lax.rem(i, 2)
    for wr in range(2):
        pl.when(parity == wr)(functools.partial(step, wr))


def _out_kernel(x_ref, y_ref, p_ref, wo_ref, pn_ref, wpg_ref, wpp_ref, fn_ref, o_ref, wo_s, wpg_s, wpp_s):
    @pl.when(pl.program_id(0) == 0)
    def _():
        wo_s[...] = wo_ref[...].astype(BF16)
        wpg_s[...] = wpg_ref[...].astype(BF16)
        wpp_s[...] = wpp_ref[...].astype(BF16)

    h = x_ref[...] + _dot(y_ref[...], wo_s[...])
    gate = _sigmoid(_dot(_rms(h, pn_ref[...]).astype(BF16), wpg_s[...]))
    h = h + gate * _dot(p_ref[...].astype(BF16), wpp_s[...])
    o_ref[...] = _rms(h, fn_ref[...])


def _const_spec(shape):
    nd = len(shape)
    return pl.BlockSpec(shape, lambda *_: (0,) * nd, pipeline_mode=pl.Buffered(1))


def _rope_freqs():
    assert ROPE_HALF == SUBLANES
    inv_freq = ROPE_THETA ** (-jnp.arange(0, ROPE_DIM, 2, dtype=F32) / ROPE_DIM)
    return jnp.broadcast_to(inv_freq[:, None], (ROPE_HALF, LANES))


def _chunk_tril(n):
    i = np.arange(n)
    same = (i[:, None] // GLA_CHUNK) == (i[None, :] // GLA_CHUNK)
    return (same & (i[None, :] <= i[:, None])).astype(np.float32)


def _layer(h, p, pos, norm_mix, w_in, sinks, w_gate_up, b_gate, gla_norm, w_out, ple_norm,
           w_ple_gate, w_ple_proj, out_norm, batch, seq):
    n = batch * seq
    o = np.cumsum((0, ATT_WIDTH, ATT_KV_WIDTH, ATT_KV_WIDTH, GLA_QK_WIDTH, GLA_QK_WIDTH, GLA_WIDTH,
                   GLA_GATE_RANK, ATT_WIDTH, GLA_WIDTH))
    col_scale = jnp.where(jnp.arange(C_GL + GLA_GATE_RANK) < ATT_WIDTH, ATT_HEAD_DIM ** -0.5, 1.0).astype(F32)
    w_head = jnp.pad((w_in[:, :o[7]] * col_scale).astype(BF16),
                     ((0, 0), (0, LANES - GLA_GATE_RANK)))
    w_tail = w_in[:, o[7]:].astype(BF16)
    wup = jnp.concatenate([w_gate_up, jnp.zeros((LANES - GLA_GATE_RANK, GLA_QK_WIDTH), w_gate_up.dtype)],
                          axis=0).astype(BF16)
    rope_freqs = _rope_freqs()
    ltri = jnp.asarray(_chunk_tril(CUM_BLK), dtype=BF16)

    params = lambda sem: pltpu.CompilerParams(dimension_semantics=sem, vmem_limit_bytes=VMEM_LIMIT)

    nt = n // TM_MIX
    in_tile = lambda i: (jnp.minimum(i, nt - 1), 0)
    out_tile = lambda i: (jnp.maximum(i - 1, 0), 0)
    two = lambda *shape: (2,) + shape
    y = pl.pallas_call(
        functools.partial(_mixer_kernel, tiles_per_seq=seq // TM_MIX),
        grid=(nt + 1,),
        in_specs=[
            pl.BlockSpec(memory_space=pltpu.SMEM),
            pl.BlockSpec((TM_MIX, D_MODEL), in_tile),
            pl.BlockSpec((1, TM_MIX // LANES, LANES), lambda i: (jnp.minimum(i, nt - 1), 0, 0)),
            _const_spec((1, D_MODEL)),
            _const_spec((D_MODEL, C_ZA)),
            _const_spec((D_MODEL, C_END - C_ZA)),
            _const_spec((LANES, GLA_QK_WIDTH)),
            _const_spec((1, GLA_QK_WIDTH)),
            _const_spec((SUBLANES, LANES)),
            _const_spec((CUM_BLK, CUM_BLK)),
            _const_spec((1, GLA_V_DIM)),
        ],
        out_specs=pl.BlockSpec((TM_MIX, ATT_WIDTH + GLA_WIDTH), out_tile),
        out_shape=jax.ShapeDtypeStruct((n, ATT_WIDTH + GLA_WIDTH), BF16),
        scratch_shapes=[
            pltpu.VMEM((TM_MIX, D_MODEL), BF16),
            pltpu.VMEM((3, TM_MIX, LANES), F32),
            pltpu.VMEM((TM_MIX, LANES), BF16),
            pltpu.VMEM((2, TM_MIX, GLA_QK_WIDTH), BF16),
            pltpu.VMEM((2, TM_MIX, GLA_QK_WIDTH), F32),
            pltpu.VMEM((ATT_GROUP * BLOCK, LANES), BF16),
            pltpu.VMEM((ATT_GROUP * BLOCK, 2 * BLOCK), F32),
            pltpu.VMEM((ATT_GROUP * BLOCK, 2 * BLOCK), BF16),
            pltpu.VMEM(two(TM_MIX, ATT_WIDTH), BF16),
            pltpu.VMEM(two(TM_MIX + BLOCK, 4 * ATT_KV_WIDTH), BF16),
            pltpu.VMEM(two(TM_MIX, GLA_QK_WIDTH), BF16),
            pltpu.VMEM(two(TM_MIX, GLA_QK_WIDTH), BF16),
            pltpu.VMEM(two(TM_MIX, GLA_QK_WIDTH), BF16),
            pltpu.VMEM(two(TM_MIX, GLA_WIDTH), BF16),
            pltpu.VMEM(two(TM_MIX, ATT_WIDTH + GLA_WIDTH), BF16),
            pltpu.VMEM(two(TM_MIX // GLA_CHUNK, GLA_QK_WIDTH), F32),
            pltpu.VMEM((GLA_HEADS, GLA_V_DIM, GLA_K_DIM), F32),
        ],
        compiler_params=params(("arbitrary",)),
        name="mixer",
    )(sinks.astype(F32), h, pos.reshape(nt, TM_MIX // LANES, LANES), norm_mix.reshape(1, D_MODEL), w_head, w_tail, wup,
      b_gate.reshape(1, GLA_QK_WIDTH), rope_freqs, ltri, gla_norm.reshape(1, GLA_V_DIM))

    row = lambda i: (i, 0)
    return pl.pallas_call(
        _out_kernel,
        grid=(n // TM_OUT,),
        in_specs=[
            pl.BlockSpec((TM_OUT, D_MODEL), row),
            pl.BlockSpec((TM_OUT, ATT_WIDTH + GLA_WIDTH), row),
            pl.BlockSpec((TM_OUT, D_PLE), row),
            _const_spec((ATT_WIDTH + GLA_WIDTH, D_MODEL)),
            _const_spec((1, D_MODEL)),
            _const_spec((D_MODEL, D_MODEL)),
            _const_spec((D_PLE, D_MODEL)),
            _const_spec((1, D_MODEL)),
        ],
        out_specs=pl.BlockSpec((TM_OUT, D_MODEL), row),
        out_shape=jax.ShapeDtypeStruct((n, D_MODEL), F32),
        scratch_shapes=[
            pltpu.VMEM((ATT_WIDTH + GLA_WIDTH, D_MODEL), BF16),
            pltpu.VMEM((D_MODEL, D_MODEL), BF16),
            pltpu.VMEM((D_PLE, D_MODEL), BF16),
        ],
        compiler_params=params(("arbitrary",)),
        name="outproj",
    )(h, y, p, w_out, ple_norm.reshape(1, D_MODEL), w_ple_gate, w_ple_proj, out_norm.reshape(1, D_MODEL))


def kernel(x, p, positions, norm_mix, w_in, attn_sinks, w_gate_up, b_gate, gla_norm, w_out, ple_norm,
           w_ple_gate, w_ple_proj, final_norm):
    batch, seq, _ = x.shape
    depth = w_in.shape[0]
    assert depth == 1, "the final RMSNorm is fused into the single layer's output kernel"
    n = batch * seq
    layer = lambda a: a.reshape(a.shape[1:])
    out = _layer(x.reshape(n, D_MODEL), p.reshape(n, D_PLE), positions.reshape(n), layer(norm_mix), layer(w_in),
                 layer(attn_sinks), layer(w_gate_up), layer(b_gate), layer(gla_norm), layer(w_out),
                 layer(ple_norm), layer(w_ple_gate), layer(w_ple_proj), final_norm, batch, seq)
    return out.reshape(batch, seq, D_MODEL)
```

```python
import functools

import jax
import jax.numpy as jnp
import numpy as np
from jax import lax
from jax.experimental import pallas as pl
from jax.experimental.pallas import tpu as pltpu

D_MODEL = 1024
D_PLE = 256
ATT_HEADS = 16
ATT_KV_HEADS = 2
ATT_HEAD_DIM = 64
ATT_WIDTH = ATT_HEADS * ATT_HEAD_DIM
ATT_KV_WIDTH = ATT_KV_HEADS * ATT_HEAD_DIM
ATT_GROUP = ATT_HEADS // ATT_KV_HEADS
BLOCK = 128
ROPE_DIM = ATT_HEAD_DIM // 4
ROPE_HALF = ROPE_DIM // 2
ROPE_THETA = 500000.0
GLA_HEADS = 4
GLA_V_DIM = 256
GLA_K_DIM = 128
GLA_WIDTH = GLA_HEADS * GLA_V_DIM
GLA_QK_WIDTH = GLA_HEADS * GLA_K_DIM
GLA_GATE_RANK = 16
GLA_GATE_NORMALIZER = 16.0
GLA_CHUNK = 64
EPS = 1e-6

LANES = 128
SUBLANES = 8
VMEM_LIMIT = 56 * 1024 * 1024

W_BLK = 256

C_Q = 0
C_KV = C_Q + ATT_WIDTH
C_GQ = C_KV + 2 * ATT_KV_WIDTH
C_GK = C_GQ + GLA_QK_WIDTH
C_GV = C_GK + GLA_QK_WIDTH
C_GL = C_GV + GLA_WIDTH
C_ZA = C_GL + W_BLK
C_ZG = C_ZA + ATT_WIDTH
C_END = C_ZG + GLA_WIDTH
D_IN_PROJ = C_END - W_BLK + GLA_GATE_RANK

TM_MIX = 512
TM_OUT = 512
CUM_BLK = 256

BF16 = jnp.bfloat16
F32 = jnp.float32


def _dot(a, b):
    return jnp.dot(a, b, preferred_element_type=F32)


def _dot_nt(a, b):
    return lax.dot_general(a, b, (((1,), (1,)), ((), ())), preferred_element_type=F32)


def _dot_tn(a, b):
    return lax.dot_general(a, b, (((0,), (0,)), ((), ())), preferred_element_type=F32)


def _rms(x, g):
    return x * lax.rsqrt(jnp.mean(x * x, axis=-1, keepdims=True) + EPS) * g


def _sigmoid(z):
    return 0.5 * (1.0 + jnp.tanh(0.5 * z))


def _inproj_pieces(x_ref, pos_ref, nmix_ref, w_ref, wup_ref, bg_ref, rope_ref, ltri_ref,
                   u_s, trig_s, glow_s, lg_s, b_s, q_s, kv_s, qd_s, ki_s, kt_s, gv_s, sg_s, bl_s):
    tm = x_ref.shape[0]

    def proj(c0, width):
        return _dot(u_s[...], w_ref[:, c0:c0 + width])

    def norm():
        u_s[...] = _rms(x_ref[...], nmix_ref[...]).astype(BF16)
        pos = pos_ref[0].astype(F32)
        ones = jnp.ones((ATT_HEAD_DIM - ROPE_DIM, LANES), F32)
        zeros = jnp.zeros((ATT_HEAD_DIM - ROPE_DIM, LANES), F32)
        zero8 = jnp.zeros((ROPE_HALF, LANES), F32)
        for r in range(tm // LANES):
            ang = rope_ref[...] * pos[r:r + 1, :]
            c8 = jnp.cos(ang)
            s8 = jnp.sin(ang)
            rows = slice(r * LANES, (r + 1) * LANES)
            trig_s[0, rows, :] = jnp.concatenate([c8, c8, ones] * 2, axis=0).T
            trig_s[1, rows, :] = jnp.concatenate([-s8, zero8, zeros] * 2, axis=0).T
            trig_s[2, rows, :] = jnp.concatenate([zero8, s8, zeros] * 2, axis=0).T

    def rope(z):
        up = pltpu.roll(z, LANES - ROPE_HALF, 1)
        dn = pltpu.roll(z, ROPE_HALF, 1)
        return z * trig_s[0] + up * trig_s[1] + dn * trig_s[2]

    def q_piece(lo, width):
        z = proj(C_Q + lo, width)
        for l in range(0, width, LANES):
            q_s[:, lo + l:lo + l + LANES] = rope(z[:, l:l + LANES]).astype(BF16)

    def kv_piece(lo, width):
        assert (lo, width) == (0, 2 * ATT_KV_WIDTH)
        z = proj(C_KV, width)
        lo_half = lax.broadcasted_iota(jnp.int32, (tm, LANES), 1) < ATT_HEAD_DIM
        k = rope(z[:, :LANES])
        k_sw = pltpu.roll(k, ATT_HEAD_DIM, 1)
        v = z[:, LANES:]
        v_sw = pltpu.roll(v, ATT_HEAD_DIM, 1)
        kv_s[BLOCK:, 0 * LANES:1 * LANES] = jnp.where(lo_half, k, k_sw).astype(BF16)
        kv_s[BLOCK:, 1 * LANES:2 * LANES] = jnp.where(lo_half, k_sw, k).astype(BF16)
        kv_s[BLOCK:, 2 * LANES:3 * LANES] = jnp.where(lo_half, v, 1.0).astype(BF16)
        kv_s[BLOCK:, 3 * LANES:4 * LANES] = jnp.where(lo_half, v_sw, 1.0).astype(BF16)

    def gv_piece(lo, width):
        gv_s[:, lo:lo + width] = proj(C_GV + lo, width).astype(BF16)

    def sg_piece(lo, width):
        z = proj(C_ZA + lo, width)
        sg_s[:, lo:lo + width] = (z * _sigmoid(z)).astype(BF16)

    def gate_low():
        glow_s[...] = proj(C_GL, LANES).astype(BF16)

    def gate_logsig():
        logit = _dot(glow_s[...], wup_ref[...]) + bg_ref[...]
        lg = (jnp.minimum(logit, 0.0) - jnp.log(1.0 + jnp.exp(-jnp.abs(logit)))) * (1.0 / GLA_GATE_NORMALIZER)
        lg_hi = lg.astype(BF16)
        lg_s[0] = lg_hi
        lg_s[1] = (lg - lg_hi.astype(F32)).astype(BF16)

    def gate_cumsum():
        ltri = ltri_ref[...]
        for r in range(0, tm, CUM_BLK):
            b = _dot(ltri, lg_s[0, r:r + CUM_BLK, :]) + _dot(ltri, lg_s[1, r:r + CUM_BLK, :])
            nb = CUM_BLK // GLA_CHUNK
            b3 = b.reshape(nb, GLA_CHUNK, GLA_QK_WIDTH)
            b_last = b3[:, GLA_CHUNK - 1:GLA_CHUNK, :]
            b_s[0, r:r + CUM_BLK, :] = b
            b_s[1, r:r + CUM_BLK, :] = (b_last - b3).reshape(CUM_BLK, GLA_QK_WIDTH)
            bl_s[r // GLA_CHUNK:r // GLA_CHUNK + nb, :] = b_last.reshape(nb, GLA_QK_WIDTH)

    def gq_piece(lo, width):
        z = proj(C_GQ + lo, width) * (GLA_K_DIM ** -0.5)
        qd_s[:, lo:lo + width] = (z * jnp.exp(b_s[0, :, lo:lo + width])).astype(BF16)

    def gk_piece(lo, width):
        z = proj(C_GK + lo, width)
        ki_s[:, lo:lo + width] = (z * jnp.exp(-b_s[0, :, lo:lo + width])).astype(BF16)
        kt_s[:, lo:lo + width] = (z * jnp.exp(b_s[1, :, lo:lo + width])).astype(BF16)

    column_pieces = {"q": q_piece, "kv": kv_piece, "gv": gv_piece, "sg": sg_piece, "gq": gq_piece, "gk": gk_piece}
    single_pieces = {"norm": norm, "gate_low": gate_low, "gate_logsig": gate_logsig, "gate_cumsum": gate_cumsum}

    def run(spec):
        if spec[0] in single_pieces:
            single_pieces[spec[0]]()
        else:
            column_pieces[spec[0]](spec[1], spec[2])
    return run


_INPROJ_ROUNDS = (
    ((("q", 0, 512),), (("q", 512, 256),)),
    ((("q", 768, 256), ("kv", 0, 256)), (("gate_low",),)),
    ((("gv", 0, 512),), (("gate_logsig",),)),
    ((("gv", 512, 512),), (("gate_cumsum",),)),
    ((("sg", 0, 512),), (("gq", 0, 256),)),
    ((("sg", 512, 512),), (("gq", 256, 256),)),
    ((("sg", 1024, 512),), (("gk", 0, 256),)),
    ((("sg", 1536, 512),), (("gk", 256, 256),)),
)


def _attn_scores(j, g, q_s, kv_s, qstack_s, score_s):
    col = lax.broadcasted_iota(jnp.int32, (BLOCK, BLOCK), 1)
    lo_half = col < ATT_HEAD_DIM
    zero = jnp.zeros((), BF16)
    kcat = kv_s[j * BLOCK:(j + 2) * BLOCK, g * LANES:(g + 1) * LANES]
    for pr in range(ATT_GROUP // 2):
        c0 = (g * (ATT_GROUP // 2) + pr) * LANES
        qp = q_s[j * BLOCK:(j + 1) * BLOCK, c0:c0 + LANES]
        qstack_s[(2 * pr) * BLOCK:(2 * pr + 1) * BLOCK, :] = jnp.where(lo_half, qp, zero)
        qstack_s[(2 * pr + 1) * BLOCK:(2 * pr + 2) * BLOCK, :] = jnp.where(lo_half, zero, qp)
    score_s[...] = _dot_nt(qstack_s[...], kcat)


def _attn_softmax(g, prev_bias, sinks_ref, score_s, p_s):
    row = lax.broadcasted_iota(jnp.int32, (BLOCK, BLOCK), 0)
    col = lax.broadcasted_iota(jnp.int32, (BLOCK, BLOCK), 1)
    causal = col <= row
    sink_terms = []
    for h in range(ATT_GROUP):
        rows = slice(h * BLOCK, (h + 1) * BLOCK)
        s_prev = score_s[rows, :BLOCK]
        if prev_bias is not None:
            s_prev = s_prev + prev_bias
        sf = jnp.where(causal, score_s[rows, BLOCK:], s_prev)
        sink = sinks_ref[g * ATT_GROUP + h]
        m = jnp.maximum(jnp.max(sf, axis=-1, keepdims=True), sink)
        e = jnp.exp(sf - m)
        sink_terms.append(jnp.exp(sink - m))
        p_s[rows, :BLOCK] = jnp.where(causal, 0.0, e).astype(BF16)
        p_s[rows, BLOCK:] = jnp.where(causal, e, 0.0).astype(BF16)
    return sink_terms


def _attn_values(j, g, sink_terms, p_s, kv_s, sg_s, y_ref):
    col = lax.broadcasted_iota(jnp.int32, (BLOCK, BLOCK), 1)
    lo_half = col < ATT_HEAD_DIM
    rows = slice(j * BLOCK, (j + 1) * BLOCK)
    vcat = kv_s[j * BLOCK:(j + 2) * BLOCK, (2 + g) * LANES:(3 + g) * LANES]
    o = _dot(p_s[...], vcat)
    for pr in range(ATT_GROUP // 2):
        he, ho = 2 * pr, 2 * pr + 1
        oe = o[he * BLOCK:(he + 1) * BLOCK]
        oo = o[ho * BLOCK:(ho + 1) * BLOCK]
        num = jnp.where(lo_half, oe, pltpu.roll(oo, ATT_HEAD_DIM, 1))
        den = jnp.where(lo_half, pltpu.roll(oe, ATT_HEAD_DIM, 1), oo) \
            + jnp.where(lo_half, sink_terms[he], sink_terms[ho])
        c0 = (g * (ATT_GROUP // 2) + pr) * LANES
        y = num * (1.0 / den) * sg_s[rows, c0:c0 + LANES].astype(F32)
        y_ref[rows, c0:c0 + LANES] = y.astype(BF16)


def _gla_independent(c, qd_s, ki_s, kt_s, gv_s):
    sl = slice(c * GLA_CHUNK, (c + 1) * GLA_CHUNK)
    out = []
    for h in range(GLA_HEADS):
        kc = slice(h * GLA_K_DIM, (h + 1) * GLA_K_DIM)
        vc = slice(h * GLA_V_DIM, (h + 1) * GLA_V_DIM)
        out.append((_dot_nt(qd_s[sl, kc], ki_s[sl, kc]), _dot_tn(gv_s[sl, vc], kt_s[sl, kc])))
    return out


def _gla_dependent(c, indep, qd_s, gv_s, sg_s, bl_s, gn_ref, st_s, y_ref):
    row = lax.broadcasted_iota(jnp.int32, (GLA_CHUNK, GLA_CHUNK), 0)
    col = lax.broadcasted_iota(jnp.int32, (GLA_CHUNK, GLA_CHUNK), 1)
    tril = col <= row
    gn = gn_ref[...]
    sl = slice(c * GLA_CHUNK, (c + 1) * GLA_CHUNK)
    for h in range(GLA_HEADS):
        kc = slice(h * GLA_K_DIM, (h + 1) * GLA_K_DIM)
        vc = slice(h * GLA_V_DIM, (h + 1) * GLA_V_DIM)
        oc = slice(ATT_WIDTH + h * GLA_V_DIM, ATT_WIDTH + (h + 1) * GLA_V_DIM)
        scores, kv_inc = indep[h]
        st = st_s[h]
        a = jnp.where(tril, scores, 0.0).astype(BF16)
        o = _dot(a, gv_s[sl, vc]) + _dot_nt(qd_s[sl, kc], st.astype(BF16))
        st_s[h] = st * jnp.exp(bl_s[c:c + 1, kc]) + kv_inc
        y = _rms(o, gn) * sg_s[sl, oc].astype(F32)
        y_ref[sl, oc] = y.astype(BF16)


def _mixer_kernel(sinks_ref, x_ref, pos_ref, nmix_ref, w_ref, wup_ref, bg_ref, rope_ref, ltri_ref, gn_ref,
                  y_ref, u_s, trig_s, glow_s, lg_s, b_s, qstack_s, score_s, p_s,
                  q_s, kv_s, qd_s, ki_s, kt_s, gv_s, sg_s, bl_s, st_s, *,
                  tiles_per_seq):
    i = pl.program_id(0)
    tm = x_ref.shape[0]
    first = lax.rem(i - 1, tiles_per_seq) == 0

    @pl.when(i == 0)
    def _():
        for s in (q_s, kv_s, qd_s, ki_s, kt_s, gv_s, sg_s, bl_s):
            s[1] = jnp.zeros(s.shape[1:], s.dtype)

    @pl.when(first | (i == 0))
    def _():
        st_s[...] = jnp.zeros_like(st_s)

    def step(wr):
        rd = 1 - wr
        inproj = _inproj_pieces(
            x_ref, pos_ref, nmix_ref, w_ref, wup_ref, bg_ref, rope_ref, ltri_ref, u_s, trig_s, glow_s, lg_s, b_s,
            q_s.at[wr], kv_s.at[wr], qd_s.at[wr], ki_s.at[wr], kt_s.at[wr], gv_s.at[wr], sg_s.at[wr],
            bl_s.at[wr])
        q_r, kv_r, sg_r, gv_r, qd_r = q_s.at[rd], kv_s.at[rd], sg_s.at[rd], gv_s.at[rd], qd_s.at[rd]
        first_bias = jnp.where(first, -jnp.inf, 0.0).astype(F32)

        inproj(("norm",))
        for r, (long_group, short_group) in enumerate(_INPROJ_ROUNDS):
            j, g = divmod(r, ATT_KV_HEADS)
            _attn_scores(j, g, q_r, kv_r, qstack_s, score_s)
            sink_terms = _attn_softmax(g, first_bias if j == 0 else None, sinks_ref, score_s, p_s)
            gla_indep = _gla_independent(r, qd_r, ki_s.at[rd], kt_s.at[rd], gv_r)
            for spec in long_group:
                inproj(spec)
            _attn_values(j, g, sink_terms, p_s, kv_r, sg_r, y_ref)
            _gla_dependent(r, gla_indep, qd_r, gv_r, sg_r, bl_s.at[rd], gn_ref, st_s, y_ref)
            for spec in short_group:
                inproj(spec)
        kv_s[wr, 0:BLOCK, :] = kv_s[rd, tm:tm + BLOCK, :]

    assert len(_INPROJ_ROUNDS) == (tm // BLOCK) * ATT_KV_HEADS == tm // GLA_CHUNK
    parity = lax.rem(i, 2)
    for wr in range(2):
        pl.when(parity == wr)(functools.partial(step, wr))


def _wprep_source_row(c):
    n_head = C_GL // W_BLK
    tail_src = C_GL + GLA_GATE_RANK
    return jnp.where(c <= n_head, c * W_BLK, tail_src + (c - n_head - 1) * W_BLK)


def _wprep_kernel(wt_ref, o_ref):
    c = pl.program_id(0)
    feat = _wprep_source_row(c) + lax.broadcasted_iota(jnp.int32, (W_BLK, 1), 0)
    scale = jnp.where(feat < ATT_WIDTH, ATT_HEAD_DIM ** -0.5, 1.0)
    padding = (c == C_GL // W_BLK) & (feat >= C_GL + GLA_GATE_RANK)
    o_ref[...] = jnp.where(padding, 0.0, wt_ref[...] * scale).T.astype(BF16)


def _out_kernel(x_ref, y_ref, p_ref, wo_ref, pn_ref, wpg_ref, wpp_ref, fn_ref, o_ref, wo_s, wpg_s, wpp_s):
    @pl.when(pl.program_id(0) == 0)
    def _():
        wo_s[...] = wo_ref[...].astype(BF16)
        wpg_s[...] = wpg_ref[...].astype(BF16)
        wpp_s[...] = wpp_ref[...].astype(BF16)

    h = x_ref[...] + _dot(y_ref[...], wo_s[...])
    gate = _sigmoid(_dot(_rms(h, pn_ref[...]).astype(BF16), wpg_s[...]))
    h = h + gate * _dot(p_ref[...].astype(BF16), wpp_s[...])
    o_ref[...] = _rms(h, fn_ref[...])


def _const_spec(shape):
    nd = len(shape)
    return pl.BlockSpec(shape, lambda *_: (0,) * nd, pipeline_mode=pl.Buffered(1))


def _rope_freqs():
    assert ROPE_HALF == SUBLANES
    inv_freq = ROPE_THETA ** (-jnp.arange(0, ROPE_DIM, 2, dtype=F32) / ROPE_DIM)
    return jnp.broadcast_to(inv_freq[:, None], (ROPE_HALF, LANES))


def _chunk_tril(n):
    i = np.arange(n)
    same = (i[:, None] // GLA_CHUNK) == (i[None, :] // GLA_CHUNK)
    return (same & (i[None, :] <= i[:, None])).astype(np.float32)


def _layer(h, p, pos, norm_mix, w_in, sinks, w_gate_up, b_gate, gla_norm, w_out, ple_norm,
           w_ple_gate, w_ple_proj, out_norm, batch, seq):
    n = batch * seq
    assert w_in.shape == (D_MODEL, D_IN_PROJ)
    w_r = pl.pallas_call(
        _wprep_kernel,
        grid=(C_END // W_BLK,),
        in_specs=[pl.BlockSpec((pl.Element(W_BLK), pl.Element(D_MODEL)), lambda c: (pl.multiple_of(_wprep_source_row(c), SUBLANES), 0))],
        out_specs=pl.BlockSpec((D_MODEL, W_BLK), lambda c: (0, c)),
        out_shape=jax.ShapeDtypeStruct((D_MODEL, C_END), BF16),
        compiler_params=pltpu.CompilerParams(dimension_semantics=("parallel",)),
        name="wprep",
    )(jnp.swapaxes(w_in, 0, 1))
    wup = jnp.concatenate([w_gate_up, jnp.zeros((LANES - GLA_GATE_RANK, GLA_QK_WIDTH), w_gate_up.dtype)],
                          axis=0).astype(BF16)
    rope_freqs = _rope_freqs()
    ltri = jnp.asarray(_chunk_tril(CUM_BLK), dtype=BF16)

    params = lambda sem: pltpu.CompilerParams(dimension_semantics=sem, vmem_limit_bytes=VMEM_LIMIT)

    nt = n // TM_MIX
    in_tile = lambda i: (jnp.minimum(i, nt - 1), 0)
    out_tile = lambda i: (jnp.maximum(i - 1, 0), 0)
    two = lambda *shape: (2,) + shape
    y = pl.pallas_call(
        functools.partial(_mixer_kernel, tiles_per_seq=seq // TM_MIX),
        grid=(nt + 1,),
        in_specs=[
            pl.BlockSpec(memory_space=pltpu.SMEM),
            pl.BlockSpec((TM_MIX, D_MODEL), in_tile),
            pl.BlockSpec((1, TM_MIX // LANES, LANES), lambda i: (jnp.minimum(i, nt - 1), 0, 0)),
            _const_spec((1, D_MODEL)),
            _const_spec((D_MODEL, C_END)),
            _const_spec((LANES, GLA_QK_WIDTH)),
            _const_spec((1, GLA_QK_WIDTH)),
            _const_spec((SUBLANES, LANES)),
            _const_spec((CUM_BLK, CUM_BLK)),
            _const_spec((1, GLA_V_DIM)),
        ],
        out_specs=pl.BlockSpec((TM_MIX, ATT_WIDTH + GLA_WIDTH), out_tile),
        out_shape=jax.ShapeDtypeStruct((n, ATT_WIDTH + GLA_WIDTH), BF16),
        scratch_shapes=[
            pltpu.VMEM((TM_MIX, D_MODEL), BF16),
            pltpu.VMEM((3, TM_MIX, LANES), F32),
            pltpu.VMEM((TM_MIX, LANES), BF16),
            pltpu.VMEM((2, TM_MIX, GLA_QK_WIDTH), BF16),
            pltpu.VMEM((2, TM_MIX, GLA_QK_WIDTH), F32),
            pltpu.VMEM((ATT_GROUP * BLOCK, LANES), BF16),
            pltpu.VMEM((ATT_GROUP * BLOCK, 2 * BLOCK), F32),
            pltpu.VMEM((ATT_GROUP * BLOCK, 2 * BLOCK), BF16),
            pltpu.VMEM(two(TM_MIX, ATT_WIDTH), BF16),
            pltpu.VMEM(two(TM_MIX + BLOCK, 4 * ATT_KV_WIDTH), BF16),
            pltpu.VMEM(two(TM_MIX, GLA_QK_WIDTH), BF16),
            pltpu.VMEM(two(TM_MIX, GLA_QK_WIDTH), BF16),
            pltpu.VMEM(two(TM_MIX, GLA_QK_WIDTH), BF16),
            pltpu.VMEM(two(TM_MIX, GLA_WIDTH), BF16),
            pltpu.VMEM(two(TM_MIX, ATT_WIDTH + GLA_WIDTH), BF16),
            pltpu.VMEM(two(TM_MIX // GLA_CHUNK, GLA_QK_WIDTH), F32),
            pltpu.VMEM((GLA_HEADS, GLA_V_DIM, GLA_K_DIM), F32),
        ],
        compiler_params=params(("arbitrary",)),
        name="mixer",
    )(sinks.astype(F32), h, pos.reshape(nt, TM_MIX // LANES, LANES), norm_mix.reshape(1, D_MODEL), w_r, wup,
      b_gate.reshape(1, GLA_QK_WIDTH), rope_freqs, ltri, gla_norm.reshape(1, GLA_V_DIM))

    row = lambda i: (i, 0)
    return pl.pallas_call(
        _out_kernel,
        grid=(n // TM_OUT,),
        in_specs=[
            pl.BlockSpec((TM_OUT, D_MODEL), row),
            pl.BlockSpec((TM_OUT, ATT_WIDTH + GLA_WIDTH), row),
            pl.BlockSpec((TM_OUT, D_PLE), row),
            _const_spec((ATT_WIDTH + GLA_WIDTH, D_MODEL)),
            _const_spec((1, D_MODEL)),
            _const_spec((D_MODEL, D_MODEL)),
            _const_spec((D_PLE, D_MODEL)),
            _const_spec((1, D_MODEL)),
        ],
        out_specs=pl.BlockSpec((TM_OUT, D_MODEL), row),
        out_shape=jax.ShapeDtypeStruct((n, D_MODEL), F32),
        scratch_shapes=[
            pltpu.VMEM((ATT_WIDTH + GLA_WIDTH, D_MODEL), BF16),
            pltpu.VMEM((D_MODEL, D_MODEL), BF16),
            pltpu.VMEM((D_PLE, D_MODEL), BF16),
        ],
        compiler_params=params(("arbitrary",)),
        name="outproj",
    )(h, y, p, w_out, ple_norm.reshape(1, D_MODEL), w_ple_gate, w_ple_proj, out_norm.reshape(1, D_MODEL))


def kernel(x, p, positions, norm_mix, w_in, attn_sinks, w_gate_up, b_gate, gla_norm, w_out, ple_norm,
           w_ple_gate, w_ple_proj, final_norm):
    batch, seq, _ = x.shape
    depth = w_in.shape[0]
    assert depth == 1, "the final RMSNorm is fused into the single layer's output kernel"
    n = batch * seq
    layer = lambda a: a.reshape(a.shape[1:])
    out = _layer(x.reshape(n, D_MODEL), p.reshape(n, D_PLE), positions.reshape(n), layer(norm_mix), layer(w_in),
                 layer(attn_sinks), layer(w_gate_up), layer(b_gate), layer(gla_norm), layer(w_out),
                 layer(ple_norm), layer(w_ple_gate), layer(w_ple_proj), final_norm, batch, seq)
    return out.reshape(batch, seq, D_MODEL)
```

```python
import functools

import jax
import jax.numpy as jnp
import numpy as np
from jax import lax
from jax.experimental import pallas as pl
from jax.experimental.pallas import tpu as pltpu

D_MODEL = 1024
D_PLE = 256
ATT_HEADS = 16
ATT_KV_HEADS = 2
ATT_HEAD_DIM = 64
ATT_WIDTH = ATT_HEADS * ATT_HEAD_DIM
ATT_KV_WIDTH = ATT_KV_HEADS * ATT_HEAD_DIM
ATT_GROUP = ATT_HEADS // ATT_KV_HEADS
BLOCK = 128
ROPE_DIM = ATT_HEAD_DIM // 4
ROPE_HALF = ROPE_DIM // 2
ROPE_THETA = 500000.0
GLA_HEADS = 4
GLA_V_DIM = 256
GLA_K_DIM = 128
GLA_WIDTH = GLA_HEADS * GLA_V_DIM
GLA_QK_WIDTH = GLA_HEADS * GLA_K_DIM
GLA_GATE_RANK = 16
GLA_GATE_NORMALIZER = 16.0
GLA_CHUNK = 64
EPS = 1e-6

LANES = 128
SUBLANES = 8
VMEM_LIMIT = 56 * 1024 * 1024

W_BLK = 256

C_Q = 0
C_KV = C_Q + ATT_WIDTH
C_GQ = C_KV + 2 * ATT_KV_WIDTH
C_GK = C_GQ + GLA_QK_WIDTH
C_GV = C_GK + GLA_QK_WIDTH
C_GL = C_GV + GLA_WIDTH
C_ZA = C_GL + W_BLK
C_ZG = C_ZA + ATT_WIDTH
C_END = C_ZG + GLA_WIDTH
D_IN_PROJ = C_END - W_BLK + GLA_GATE_RANK

TM_MIX = 512
TM_OUT = 512
CUM_BLK = 256

BF16 = jnp.bfloat16
F32 = jnp.float32


def _dot(a, b):
    return jnp.dot(a, b, preferred_element_type=F32)


def _dot_nt(a, b):
    return lax.dot_general(a, b, (((1,), (1,)), ((), ())), preferred_element_type=F32)


def _dot_tn(a, b):
    return lax.dot_general(a, b, (((0,), (0,)), ((), ())), preferred_element_type=F32)


def _rms(x, g):
    return x * lax.rsqrt(jnp.mean(x * x, axis=-1, keepdims=True) + EPS) * g


def _sigmoid(z):
    return 0.5 * (1.0 + jnp.tanh(0.5 * z))


def _inproj_pieces(x_ref, pos_ref, nmix_ref, w_ref, wup_ref, bg_ref, rope_ref, ltri_ref,
                   u_s, trig_s, glow_s, lg_s, b_s, q_s, kv_s, qd_s, ki_s, kt_s, gv_s, sg_s, dk_s):
    tm = x_ref.shape[0]

    def proj(c0, width):
        return _dot(u_s[...], w_ref[:, c0:c0 + width])

    def norm():
        u_s[...] = _rms(x_ref[...], nmix_ref[...]).astype(BF16)
        pos = pos_ref[0].astype(F32)
        ones = jnp.ones((ATT_HEAD_DIM - ROPE_DIM, LANES), F32)
        zeros = jnp.zeros((ATT_HEAD_DIM - ROPE_DIM, LANES), F32)
        zero8 = jnp.zeros((ROPE_HALF, LANES), F32)
        for r in range(tm // LANES):
            ang = rope_ref[...] * pos[r:r + 1, :]
            c8 = jnp.cos(ang)
            s8 = jnp.sin(ang)
            rows = slice(r * LANES, (r + 1) * LANES)
            trig_s[0, rows, :] = jnp.concatenate([c8, c8, ones] * 2, axis=0).T
            trig_s[1, rows, :] = jnp.concatenate([-s8, zero8, zeros] * 2, axis=0).T
            trig_s[2, rows, :] = jnp.concatenate([zero8, s8, zeros] * 2, axis=0).T

    def rope(z):
        up = pltpu.roll(z, LANES - ROPE_HALF, 1)
        dn = pltpu.roll(z, ROPE_HALF, 1)
        return z * trig_s[0] + up * trig_s[1] + dn * trig_s[2]

    def q_piece(lo, width):
        z = proj(C_Q + lo, width)
        for l in range(0, width, LANES):
            q_s[:, lo + l:lo + l + LANES] = rope(z[:, l:l + LANES]).astype(BF16)

    def kv_piece(lo, width):
        assert (lo, width) == (0, 2 * ATT_KV_WIDTH)
        z = proj(C_KV, width)
        lo_half = lax.broadcasted_iota(jnp.int32, (tm, LANES), 1) < ATT_HEAD_DIM
        k = rope(z[:, :LANES])
        k_sw = pltpu.roll(k, ATT_HEAD_DIM, 1)
        v = z[:, LANES:]
        v_sw = pltpu.roll(v, ATT_HEAD_DIM, 1)
        kv_s[BLOCK:, 0 * LANES:1 * LANES] = jnp.where(lo_half, k, k_sw).astype(BF16)
        kv_s[BLOCK:, 1 * LANES:2 * LANES] = jnp.where(lo_half, k_sw, k).astype(BF16)
        kv_s[BLOCK:, 2 * LANES:3 * LANES] = jnp.where(lo_half, v, 1.0).astype(BF16)
        kv_s[BLOCK:, 3 * LANES:4 * LANES] = jnp.where(lo_half, v_sw, 1.0).astype(BF16)

    def gv_piece(lo, width):
        gv_s[:, lo:lo + width] = proj(C_GV + lo, width).astype(BF16)

    def sg_piece(lo, width):
        z = proj(C_ZA + lo, width)
        sg_s[:, lo:lo + width] = (z * _sigmoid(z)).astype(BF16)

    def gate_low():
        glow_s[...] = proj(C_GL, LANES).astype(BF16)

    def gate_logsig():
        logit = _dot(glow_s[...], wup_ref[...]) + bg_ref[...]
        lg = (jnp.minimum(logit, 0.0) - jnp.log(1.0 + jnp.exp(-jnp.abs(logit)))) * (1.0 / GLA_GATE_NORMALIZER)
        lg_hi = lg.astype(BF16)
        lg_s[0] = lg_hi
        lg_s[1] = (lg - lg_hi.astype(F32)).astype(BF16)

    def gate_cumsum():
        ltri = ltri_ref[...]
        nb = CUM_BLK // GLA_CHUNK
        chunk_totals = []
        for r in range(0, tm, CUM_BLK):
            b = _dot(ltri, lg_s[0, r:r + CUM_BLK, :]) + _dot(ltri, lg_s[1, r:r + CUM_BLK, :])
            b3 = b.reshape(nb, GLA_CHUNK, GLA_QK_WIDTH)
            b_last = b3[:, GLA_CHUNK - 1:GLA_CHUNK, :]
            b_s[0, r:r + CUM_BLK, :] = b
            b_s[1, r:r + CUM_BLK, :] = (b_last - b3).reshape(CUM_BLK, GLA_QK_WIDTH)
            chunk_totals.append(b_last.reshape(nb, GLA_QK_WIDTH))
        pad = jnp.zeros((LANES - tm // GLA_CHUNK, GLA_QK_WIDTH), F32)
        decay = jnp.exp(jnp.concatenate(chunk_totals + [pad], axis=0))
        for h in range(GLA_HEADS):
            kc = slice(h * GLA_K_DIM, (h + 1) * GLA_K_DIM)
            dk_s[kc, :] = decay[:, kc].T

    def gq_piece(lo, width):
        z = proj(C_GQ + lo, width) * (GLA_K_DIM ** -0.5)
        qd_s[:, lo:lo + width] = (z * jnp.exp(b_s[0, :, lo:lo + width])).astype(BF16)

    def gk_piece(lo, width):
        z = proj(C_GK + lo, width)
        ki_s[:, lo:lo + width] = (z * jnp.exp(-b_s[0, :, lo:lo + width])).astype(BF16)
        kt_s[:, lo:lo + width] = (z * jnp.exp(b_s[1, :, lo:lo + width])).astype(BF16)

    column_pieces = {"q": q_piece, "kv": kv_piece, "gv": gv_piece, "sg": sg_piece, "gq": gq_piece, "gk": gk_piece}
    single_pieces = {"norm": norm, "gate_low": gate_low, "gate_logsig": gate_logsig, "gate_cumsum": gate_cumsum}

    def run(spec):
        if spec[0] in single_pieces:
            single_pieces[spec[0]]()
        else:
            column_pieces[spec[0]](spec[1], spec[2])
    return run


_INPROJ_ROUNDS = (
    ((("q", 0, 512),), (("q", 512, 256),)),
    ((("q", 768, 256), ("kv", 0, 256)), (("gate_low",),)),
    ((("gv", 0, 512),), (("gate_logsig",),)),
    ((("gv", 512, 512),), (("gate_cumsum",),)),
    ((("sg", 0, 512),), (("gq", 0, 256),)),
    ((("sg", 512, 512),), (("gq", 256, 256),)),
    ((("sg", 1024, 512),), (("gk", 0, 256),)),
    ((("sg", 1536, 512),), (("gk", 256, 256),)),
)


def _attn_scores(j, g, q_s, kv_s, qstack_s, score_s):
    col = lax.broadcasted_iota(jnp.int32, (BLOCK, BLOCK), 1)
    lo_half = col < ATT_HEAD_DIM
    zero = jnp.zeros((), BF16)
    kcat = kv_s[j * BLOCK:(j + 2) * BLOCK, g * LANES:(g + 1) * LANES]
    for pr in range(ATT_GROUP // 2):
        c0 = (g * (ATT_GROUP // 2) + pr) * LANES
        qp = q_s[j * BLOCK:(j + 1) * BLOCK, c0:c0 + LANES]
        qstack_s[(2 * pr) * BLOCK:(2 * pr + 1) * BLOCK, :] = jnp.where(lo_half, qp, zero)
        qstack_s[(2 * pr + 1) * BLOCK:(2 * pr + 2) * BLOCK, :] = jnp.where(lo_half, zero, qp)
    score_s[...] = _dot_nt(qstack_s[...], kcat)


def _attn_softmax(g, prev_bias, sinks_ref, score_s, p_s):
    row = lax.broadcasted_iota(jnp.int32, (BLOCK, BLOCK), 0)
    col = lax.broadcasted_iota(jnp.int32, (BLOCK, BLOCK), 1)
    causal = col <= row
    sink_terms = []
    for h in range(ATT_GROUP):
        rows = slice(h * BLOCK, (h + 1) * BLOCK)
        s_prev = score_s[rows, :BLOCK]
        if prev_bias is not None:
            s_prev = s_prev + prev_bias
        sf = jnp.where(causal, score_s[rows, BLOCK:], s_prev)
        sink = sinks_ref[g * ATT_GROUP + h]
        m = jnp.maximum(jnp.max(sf, axis=-1, keepdims=True), sink)
        e = jnp.exp(sf - m)
        sink_terms.append(jnp.exp(sink - m))
        p_s[rows, :BLOCK] = jnp.where(causal, 0.0, e).astype(BF16)
        p_s[rows, BLOCK:] = jnp.where(causal, e, 0.0).astype(BF16)
    return sink_terms


def _attn_values(j, g, sink_terms, p_s, kv_s, sg_s, y_ref):
    col = lax.broadcasted_iota(jnp.int32, (BLOCK, BLOCK), 1)
    lo_half = col < ATT_HEAD_DIM
    rows = slice(j * BLOCK, (j + 1) * BLOCK)
    vcat = kv_s[j * BLOCK:(j + 2) * BLOCK, (2 + g) * LANES:(3 + g) * LANES]
    o = _dot(p_s[...], vcat)
    for pr in range(ATT_GROUP // 2):
        he, ho = 2 * pr, 2 * pr + 1
        oe = o[he * BLOCK:(he + 1) * BLOCK]
        oo = o[ho * BLOCK:(ho + 1) * BLOCK]
        num = jnp.where(lo_half, oe, pltpu.roll(oo, ATT_HEAD_DIM, 1))
        den = jnp.where(lo_half, pltpu.roll(oe, ATT_HEAD_DIM, 1), oo) \
            + jnp.where(lo_half, sink_terms[he], sink_terms[ho])
        c0 = (g * (ATT_GROUP // 2) + pr) * LANES
        y = num * (1.0 / den) * sg_s[rows, c0:c0 + LANES].astype(F32)
        y_ref[rows, c0:c0 + LANES] = y.astype(BF16)


def _gla_independent(c, qd_s, ki_s, kt_s, gv_s):
    sl = slice(c * GLA_CHUNK, (c + 1) * GLA_CHUNK)
    out = []
    for h in range(GLA_HEADS):
        kc = slice(h * GLA_K_DIM, (h + 1) * GLA_K_DIM)
        vc = slice(h * GLA_V_DIM, (h + 1) * GLA_V_DIM)
        out.append((_dot_nt(qd_s[sl, kc], ki_s[sl, kc]), _dot_tn(kt_s[sl, kc], gv_s[sl, vc])))
    return out


def _gla_dependent(c, indep, qd_s, gv_s, sg_s, dk_s, gn_ref, st_s, y_ref):
    row = lax.broadcasted_iota(jnp.int32, (GLA_CHUNK, GLA_CHUNK), 0)
    col = lax.broadcasted_iota(jnp.int32, (GLA_CHUNK, GLA_CHUNK), 1)
    tril = col <= row
    gn = gn_ref[...]
    sl = slice(c * GLA_CHUNK, (c + 1) * GLA_CHUNK)
    for h in range(GLA_HEADS):
        kc = slice(h * GLA_K_DIM, (h + 1) * GLA_K_DIM)
        vc = slice(h * GLA_V_DIM, (h + 1) * GLA_V_DIM)
        oc = slice(ATT_WIDTH + h * GLA_V_DIM, ATT_WIDTH + (h + 1) * GLA_V_DIM)
        scores, kv_inc = indep[h]
        st = st_s[h]
        a = jnp.where(tril, scores, 0.0).astype(BF16)
        o = _dot(jnp.concatenate([qd_s[sl, kc], a], axis=1),
                 jnp.concatenate([st.astype(BF16), gv_s[sl, vc]], axis=0))
        st_s[h] = st * dk_s[kc, c:c + 1] + kv_inc
        y = _rms(o, gn) * sg_s[sl, oc].astype(F32)
        y_ref[sl, oc] = y.astype(BF16)


def _mixer_kernel(sinks_ref, x_ref, pos_ref, nmix_ref, w_ref, wup_ref, bg_ref, rope_ref, ltri_ref, gn_ref,
                  y_ref, u_s, trig_s, glow_s, lg_s, b_s, qstack_s, score_s, p_s,
                  q_s, kv_s, qd_s, ki_s, kt_s, gv_s, sg_s, dk_s, st_s, *,
                  tiles_per_seq):
    i = pl.program_id(0)
    tm = x_ref.shape[0]
    first = lax.rem(i - 1, tiles_per_seq) == 0

    @pl.when(i == 0)
    def _():
        for s in (q_s, kv_s, qd_s, ki_s, kt_s, gv_s, sg_s, dk_s):
            s[1] = jnp.zeros(s.shape[1:], s.dtype)

    @pl.when(first | (i == 0))
    def _():
        st_s[...] = jnp.zeros_like(st_s)

    def step(wr):
        rd = 1 - wr
        inproj = _inproj_pieces(
            x_ref, pos_ref, nmix_ref, w_ref, wup_ref, bg_ref, rope_ref, ltri_ref, u_s, trig_s, glow_s, lg_s, b_s,
            q_s.at[wr], kv_s.at[wr], qd_s.at[wr], ki_s.at[wr], kt_s.at[wr], gv_s.at[wr], sg_s.at[wr],
            dk_s.at[wr])
        q_r, kv_r, sg_r, gv_r, qd_r = q_s.at[rd], kv_s.at[rd], sg_s.at[rd], gv_s.at[rd], qd_s.at[rd]
        first_bias = jnp.where(first, -jnp.inf, 0.0).astype(F32)

        inproj(("norm",))
        for r, (long_group, short_group) in enumerate(_INPROJ_ROUNDS):
            j, g = divmod(r, ATT_KV_HEADS)
            _attn_scores(j, g, q_r, kv_r, qstack_s, score_s)
            sink_terms = _attn_softmax(g, first_bias if j == 0 else None, sinks_ref, score_s, p_s)
            gla_indep = _gla_independent(r, qd_r, ki_s.at[rd], kt_s.at[rd], gv_r)
            for spec in long_group:
                inproj(spec)
            _attn_values(j, g, sink_terms, p_s, kv_r, sg_r, y_ref)
            _gla_dependent(r, gla_indep, qd_r, gv_r, sg_r, dk_s.at[rd], gn_ref, st_s, y_ref)
            for spec in short_group:
                inproj(spec)
        kv_s[wr, 0:BLOCK, :] = kv_s[rd, tm:tm + BLOCK, :]

    assert len(_INPROJ_ROUNDS) == (tm // BLOCK) * ATT_KV_HEADS == tm // GLA_CHUNK
    parity = lax.rem(i, 2)
    for wr in range(2):
        pl.when(parity == wr)(functools.partial(step, wr))


def _wprep_source_row(c):
    n_head = C_GL // W_BLK
    tail_src = C_GL + GLA_GATE_RANK
    return jnp.where(c <= n_head, c * W_BLK, tail_src + (c - n_head - 1) * W_BLK)


def _wprep_kernel(wt_ref, o_ref):
    c = pl.program_id(0)
    feat = _wprep_source_row(c) + lax.broadcasted_iota(jnp.int32, (W_BLK, 1), 0)
    scale = jnp.where(feat < ATT_WIDTH, ATT_HEAD_DIM ** -0.5, 1.0)
    padding = (c == C_GL // W_BLK) & (feat >= C_GL + GLA_GATE_RANK)
    o_ref[...] = jnp.where(padding, 0.0, wt_ref[...] * scale).T.astype(BF16)


def _out_kernel(x_ref, y_ref, p_ref, wo_ref, pn_ref, wpg_ref, wpp_ref, fn_ref, o_ref, wo_s, wpg_s, wpp_s):
    @pl.when(pl.program_id(0) == 0)
    def _():
        wo_s[...] = wo_ref[...].astype(BF16)
        wpg_s[...] = wpg_ref[...].astype(BF16)
        wpp_s[...] = wpp_ref[...].astype(BF16)

    h = x_ref[...] + _dot(y_ref[...], wo_s[...])
    gate = _sigmoid(_dot(_rms(h, pn_ref[...]).astype(BF16), wpg_s[...]))
    h = h + gate * _dot(p_ref[...].astype(BF16), wpp_s[...])
    o_ref[...] = _rms(h, fn_ref[...])


def _const_spec(shape):
    nd = len(shape)
    return pl.BlockSpec(shape, lambda *_: (0,) * nd, pipeline_mode=pl.Buffered(1))


def _rope_freqs():
    assert ROPE_HALF == SUBLANES
    inv_freq = ROPE_THETA ** (-jnp.arange(0, ROPE_DIM, 2, dtype=F32) / ROPE_DIM)
    return jnp.broadcast_to(inv_freq[:, None], (ROPE_HALF, LANES))


def _chunk_tril(n):
    i = np.arange(n)
    same = (i[:, None] // GLA_CHUNK) == (i[None, :] // GLA_CHUNK)
    return (same & (i[None, :] <= i[:, None])).astype(np.float32)


def _layer(h, p, pos, norm_mix, w_in, sinks, w_gate_up, b_gate, gla_norm, w_out, ple_norm,
           w_ple_gate, w_ple_proj, out_norm, batch, seq):
    n = batch * seq
    assert w_in.shape == (D_MODEL, D_IN_PROJ)
    w_r = pl.pallas_call(
        _wprep_kernel,
        grid=(C_END // W_BLK,),
        in_specs=[pl.BlockSpec((pl.Element(W_BLK), pl.Element(D_MODEL)), lambda c: (pl.multiple_of(_wprep_source_row(c), SUBLANES), 0))],
        out_specs=pl.BlockSpec((D_MODEL, W_BLK), lambda c: (0, c)),
        out_shape=jax.ShapeDtypeStruct((D_MODEL, C_END), BF16),
        compiler_params=pltpu.CompilerParams(dimension_semantics=("parallel",)),
        name="wprep",
    )(jnp.swapaxes(w_in, 0, 1))
    wup = jnp.concatenate([w_gate_up, jnp.zeros((LANES - GLA_GATE_RANK, GLA_QK_WIDTH), w_gate_up.dtype)],
                          axis=0).astype(BF16)
    rope_freqs = _rope_freqs()
    ltri = jnp.asarray(_chunk_tril(CUM_BLK), dtype=BF16)

    params = lambda sem: pltpu.CompilerParams(dimension_semantics=sem, vmem_limit_bytes=VMEM_LIMIT)

    nt = n // TM_MIX
    in_tile = lambda i: (jnp.minimum(i, nt - 1), 0)
    out_tile = lambda i: (jnp.maximum(i - 1, 0), 0)
    two = lambda *shape: (2,) + shape
    y = pl.pallas_call(
        functools.partial(_mixer_kernel, tiles_per_seq=seq // TM_MIX),
        grid=(nt + 1,),
        in_specs=[
            pl.BlockSpec(memory_space=pltpu.SMEM),
            pl.BlockSpec((TM_MIX, D_MODEL), in_tile),
            pl.BlockSpec((1, TM_MIX // LANES, LANES), lambda i: (jnp.minimum(i, nt - 1), 0, 0)),
            _const_spec((1, D_MODEL)),
            _const_spec((D_MODEL, C_END)),
            _const_spec((LANES, GLA_QK_WIDTH)),
            _const_spec((1, GLA_QK_WIDTH)),
            _const_spec((SUBLANES, LANES)),
            _const_spec((CUM_BLK, CUM_BLK)),
            _const_spec((1, GLA_V_DIM)),
        ],
        out_specs=pl.BlockSpec((TM_MIX, ATT_WIDTH + GLA_WIDTH), out_tile),
        out_shape=jax.ShapeDtypeStruct((n, ATT_WIDTH + GLA_WIDTH), BF16),
        scratch_shapes=[
            pltpu.VMEM((TM_MIX, D_MODEL), BF16),
            pltpu.VMEM((3, TM_MIX, LANES), F32),
            pltpu.VMEM((TM_MIX, LANES), BF16),
            pltpu.VMEM((2, TM_MIX, GLA_QK_WIDTH), BF16),
            pltpu.VMEM((2, TM_MIX, GLA_QK_WIDTH), F32),
            pltpu.VMEM((ATT_GROUP * BLOCK, LANES), BF16),
            pltpu.VMEM((ATT_GROUP * BLOCK, 2 * BLOCK), F32),
            pltpu.VMEM((ATT_GROUP * BLOCK, 2 * BLOCK), BF16),
            pltpu.VMEM(two(TM_MIX, ATT_WIDTH), BF16),
            pltpu.VMEM(two(TM_MIX + BLOCK, 4 * ATT_KV_WIDTH), BF16),
            pltpu.VMEM(two(TM_MIX, GLA_QK_WIDTH), BF16),
            pltpu.VMEM(two(TM_MIX, GLA_QK_WIDTH), BF16),
            pltpu.VMEM(two(TM_MIX, GLA_QK_WIDTH), BF16),
            pltpu.VMEM(two(TM_MIX, GLA_WIDTH), BF16),
            pltpu.VMEM(two(TM_MIX, ATT_WIDTH + GLA_WIDTH), BF16),
            pltpu.VMEM(two(GLA_QK_WIDTH, LANES), F32),
            pltpu.VMEM((GLA_HEADS, GLA_K_DIM, GLA_V_DIM), F32),
        ],
        compiler_params=params(("arbitrary",)),
        name="mixer",
    )(sinks.astype(F32), h, pos.reshape(nt, TM_MIX // LANES, LANES), norm_mix.reshape(1, D_MODEL), w_r, wup,
      b_gate.reshape(1, GLA_QK_WIDTH), rope_freqs, ltri, gla_norm.reshape(1, GLA_V_DIM))

    row = lambda i: (i, 0)
    return pl.pallas_call(
        _out_kernel,
        grid=(n // TM_OUT,),
        in_specs=[
            pl.BlockSpec((TM_OUT, D_MODEL), row),
            pl.BlockSpec((TM_OUT, ATT_WIDTH + GLA_WIDTH), row),
            pl.BlockSpec((TM_OUT, D_PLE), row),
            _const_spec((ATT_WIDTH + GLA_WIDTH, D_MODEL)),
            _const_spec((1, D_MODEL)),
            _const_spec((D_MODEL, D_MODEL)),
            _const_spec((D_PLE, D_MODEL)),
            _const_spec((1, D_MODEL)),
        ],
        out_specs=pl.BlockSpec((TM_OUT, D_MODEL), row),
        out_shape=jax.ShapeDtypeStruct((n, D_MODEL), F32),
        scratch_shapes=[
            pltpu.VMEM((ATT_WIDTH + GLA_WIDTH, D_MODEL), BF16),
            pltpu.VMEM((D_MODEL, D_MODEL), BF16),
            pltpu.VMEM((D_PLE, D_MODEL), BF16),
        ],
        compiler_params=params(("arbitrary",)),
        name="outproj",
    )(h, y, p, w_out, ple_norm.reshape(1, D_MODEL), w_ple_gate, w_ple_proj, out_norm.reshape(1, D_MODEL))


def kernel(x, p, positions, norm_mix, w_in, attn_sinks, w_gate_up, b_gate, gla_norm, w_out, ple_norm,
           w_ple_gate, w_ple_proj, final_norm):
    batch, seq, _ = x.shape
    depth = w_in.shape[0]
    assert depth == 1, "the final RMSNorm is fused into the single layer's output kernel"
    n = batch * seq
    layer = lambda a: a.reshape(a.shape[1:])
    out = _layer(x.reshape(n, D_MODEL), p.reshape(n, D_PLE), positions.reshape(n), layer(norm_mix), layer(w_in),
                 layer(attn_sinks), layer(w_gate_up), layer(b_gate), layer(gla_norm), layer(w_out),
                 layer(ple_norm), layer(w_ple_gate), layer(w_ple_proj), final_norm, batch, seq)
    return out.reshape(batch, seq, D_MODEL)
```

```python
import functools

import jax
import jax.numpy as jnp
import numpy as np
from jax import lax
from jax.experimental import pallas as pl
from jax.experimental.pallas import tpu as pltpu

D_MODEL = 1024
D_PLE = 256
ATT_HEADS = 16
ATT_KV_HEADS = 2
ATT_HEAD_DIM = 64
ATT_WIDTH = ATT_HEADS * ATT_HEAD_DIM
ATT_KV_WIDTH = ATT_KV_HEADS * ATT_HEAD_DIM
ATT_GROUP = ATT_HEADS // ATT_KV_HEADS
BLOCK = 128
ROPE_DIM = ATT_HEAD_DIM // 4
ROPE_HALF = ROPE_DIM // 2
ROPE_THETA = 500000.0
GLA_HEADS = 4
GLA_V_DIM = 256
GLA_K_DIM = 128
GLA_WIDTH = GLA_HEADS * GLA_V_DIM
GLA_QK_WIDTH = GLA_HEADS * GLA_K_DIM
GLA_GATE_RANK = 16
GLA_GATE_NORMALIZER = 16.0
GLA_CHUNK = 64
EPS = 1e-6

LANES = 128
SUBLANES = 8
VMEM_LIMIT = 56 * 1024 * 1024

W_BLK = 256
WPREP_BLKS = 2

C_Q = 0
C_KV = C_Q + ATT_WIDTH
C_GQ = C_KV + 2 * ATT_KV_WIDTH
C_GK = C_GQ + GLA_QK_WIDTH
C_GV = C_GK + GLA_QK_WIDTH
C_GL = C_GV + GLA_WIDTH
C_ZA = C_GL + W_BLK
C_ZG = C_ZA + ATT_WIDTH
C_END = C_ZG + GLA_WIDTH
D_IN_PROJ = C_END - W_BLK + GLA_GATE_RANK

TM_MIX = 512
TM_OUT = 512
CUM_BLK = 256

BF16 = jnp.bfloat16
F32 = jnp.float32


def _dot(a, b):
    return jnp.dot(a, b, preferred_element_type=F32)


def _dot_nt(a, b):
    return lax.dot_general(a, b, (((1,), (1,)), ((), ())), preferred_element_type=F32)


def _dot_tn(a, b):
    return lax.dot_general(a, b, (((0,), (0,)), ((), ())), preferred_element_type=F32)


def _rms(x, g):
    return x * lax.rsqrt(jnp.mean(x * x, axis=-1, keepdims=True) + EPS) * g


def _sigmoid(z):
    return 0.5 * (1.0 + jnp.tanh(0.5 * z))


def _inproj_pieces(x_ref, pos_ref, nmix_ref, w_ref, wup_ref, bg_ref, rope_ref, ltri_ref,
                   u_s, trig_s, glow_s, lg_s, b_s, q_s, kv_s, qd_s, ki_s, kt_s, gv_s, sg_s, dk_s):
    tm = x_ref.shape[0]

    def proj(c0, width):
        return _dot(u_s[...], w_ref[:, c0:c0 + width])

    def norm():
        u_s[...] = _rms(x_ref[...], nmix_ref[...]).astype(BF16)
        pos = pos_ref[0].astype(F32)
        ones = jnp.ones((ATT_HEAD_DIM - ROPE_DIM, LANES), F32)
        zeros = jnp.zeros((ATT_HEAD_DIM - ROPE_DIM, LANES), F32)
        zero8 = jnp.zeros((ROPE_HALF, LANES), F32)
        for r in range(tm // LANES):
            ang = rope_ref[...] * pos[r:r + 1, :]
            c8 = jnp.cos(ang)
            s8 = jnp.sin(ang)
            rows = slice(r * LANES, (r + 1) * LANES)
            trig_s[0, rows, :] = jnp.concatenate([c8, c8, ones] * 2, axis=0).T
            trig_s[1, rows, :] = jnp.concatenate([-s8, zero8, zeros] * 2, axis=0).T
            trig_s[2, rows, :] = jnp.concatenate([zero8, s8, zeros] * 2, axis=0).T

    def rope(z):
        up = pltpu.roll(z, LANES - ROPE_HALF, 1)
        dn = pltpu.roll(z, ROPE_HALF, 1)
        return z * trig_s[0] + up * trig_s[1] + dn * trig_s[2]

    def q_piece(lo, width):
        z = proj(C_Q + lo, width)
        for l in range(0, width, LANES):
            q_s[:, lo + l:lo + l + LANES] = rope(z[:, l:l + LANES]).astype(BF16)

    def kv_piece(lo, width):
        assert (lo, width) == (0, 2 * ATT_KV_WIDTH)
        z = proj(C_KV, width)
        lo_half = lax.broadcasted_iota(jnp.int32, (tm, LANES), 1) < ATT_HEAD_DIM
        k = rope(z[:, :LANES])
        k_sw = pltpu.roll(k, ATT_HEAD_DIM, 1)
        v = z[:, LANES:]
        v_sw = pltpu.roll(v, ATT_HEAD_DIM, 1)
        kv_s[BLOCK:, 0 * LANES:1 * LANES] = jnp.where(lo_half, k, k_sw).astype(BF16)
        kv_s[BLOCK:, 1 * LANES:2 * LANES] = jnp.where(lo_half, k_sw, k).astype(BF16)
        kv_s[BLOCK:, 2 * LANES:3 * LANES] = jnp.where(lo_half, v, 1.0).astype(BF16)
        kv_s[BLOCK:, 3 * LANES:4 * LANES] = jnp.where(lo_half, v_sw, 1.0).astype(BF16)

    def gv_piece(lo, width):
        gv_s[:, lo:lo + width] = proj(C_GV + lo, width).astype(BF16)

    def sg_piece(lo, width):
        z = proj(C_ZA + lo, width)
        sg_s[:, lo:lo + width] = (z * _sigmoid(z)).astype(BF16)

    def gate_low():
        glow_s[...] = proj(C_GL, LANES).astype(BF16)

    def gate_logsig():
        logit = _dot(glow_s[...], wup_ref[...]) + bg_ref[...]
        lg = (jnp.minimum(logit, 0.0) - jnp.log(1.0 + jnp.exp(-jnp.abs(logit)))) * (1.0 / GLA_GATE_NORMALIZER)
        lg_hi = lg.astype(BF16)
        lg_s[0] = lg_hi
        lg_s[1] = (lg - lg_hi.astype(F32)).astype(BF16)

    def gate_cumsum():
        ltri = ltri_ref[...]
        nb = CUM_BLK // GLA_CHUNK
        chunk_totals = []
        for r in range(0, tm, CUM_BLK):
            b = _dot(ltri, lg_s[0, r:r + CUM_BLK, :]) + _dot(ltri, lg_s[1, r:r + CUM_BLK, :])
            b3 = b.reshape(nb, GLA_CHUNK, GLA_QK_WIDTH)
            b_last = b3[:, GLA_CHUNK - 1:GLA_CHUNK, :]
            b_s[0, r:r + CUM_BLK, :] = b
            b_s[1, r:r + CUM_BLK, :] = (b_last - b3).reshape(CUM_BLK, GLA_QK_WIDTH)
            chunk_totals.append(b_last.reshape(nb, GLA_QK_WIDTH))
        pad = jnp.zeros((LANES - tm // GLA_CHUNK, GLA_QK_WIDTH), F32)
        decay = jnp.exp(jnp.concatenate(chunk_totals + [pad], axis=0))
        for h in range(GLA_HEADS):
            kc = slice(h * GLA_K_DIM, (h + 1) * GLA_K_DIM)
            dk_s[kc, :] = decay[:, kc].T

    def gq_piece(lo, width):
        z = proj(C_GQ + lo, width) * (GLA_K_DIM ** -0.5)
        qd_s[:, lo:lo + width] = (z * jnp.exp(b_s[0, :, lo:lo + width])).astype(BF16)

    def gk_piece(lo, width):
        z = proj(C_GK + lo, width)
        ki_s[:, lo:lo + width] = (z * jnp.exp(-b_s[0, :, lo:lo + width])).astype(BF16)
        kt_s[:, lo:lo + width] = (z * jnp.exp(b_s[1, :, lo:lo + width])).astype(BF16)

    column_pieces = {"q": q_piece, "kv": kv_piece, "gv": gv_piece, "sg": sg_piece, "gq": gq_piece, "gk": gk_piece}
    single_pieces = {"norm": norm, "gate_low": gate_low, "gate_logsig": gate_logsig, "gate_cumsum": gate_cumsum}

    def run(spec):
        if spec[0] in single_pieces:
            single_pieces[spec[0]]()
        else:
            column_pieces[spec[0]](spec[1], spec[2])
    return run


_INPROJ_ROUNDS = (
    ((("q", 0, 512),), (("q", 512, 256),)),
    ((("q", 768, 256), ("kv", 0, 256)), (("gate_low",),)),
    ((("gv", 0, 512),), (("gate_logsig",),)),
    ((("gv", 512, 512),), (("gate_cumsum",),)),
    ((("sg", 0, 512),), (("gq", 0, 256),)),
    ((("sg", 512, 512),), (("gq", 256, 256),)),
    ((("sg", 1024, 512),), (("gk", 0, 256),)),
    ((("sg", 1536, 512),), (("gk", 256, 256),)),
)


def _attn_scores(j, g, q_s, kv_s, qstack_s, score_s):
    col = lax.broadcasted_iota(jnp.int32, (BLOCK, BLOCK), 1)
    lo_half = col < ATT_HEAD_DIM
    zero = jnp.zeros((), BF16)
    kcat = kv_s[j * BLOCK:(j + 2) * BLOCK, g * LANES:(g + 1) * LANES]
    for pr in range(ATT_GROUP // 2):
        c0 = (g * (ATT_GROUP // 2) + pr) * LANES
        qp = q_s[j * BLOCK:(j + 1) * BLOCK, c0:c0 + LANES]
        qstack_s[(2 * pr) * BLOCK:(2 * pr + 1) * BLOCK, :] = jnp.where(lo_half, qp, zero)
        qstack_s[(2 * pr + 1) * BLOCK:(2 * pr + 2) * BLOCK, :] = jnp.where(lo_half, zero, qp)
    score_s[...] = _dot_nt(qstack_s[...], kcat)


def _attn_softmax(g, prev_bias, sinks_ref, score_s, p_s):
    row = lax.broadcasted_iota(jnp.int32, (BLOCK, BLOCK), 0)
    col = lax.broadcasted_iota(jnp.int32, (BLOCK, BLOCK), 1)
    causal = col <= row
    sink_terms = []
    for h in range(ATT_GROUP):
        rows = slice(h * BLOCK, (h + 1) * BLOCK)
        s_prev = score_s[rows, :BLOCK]
        if prev_bias is not None:
            s_prev = s_prev + prev_bias
        sf = jnp.where(causal, score_s[rows, BLOCK:], s_prev)
        sink = sinks_ref[g * ATT_GROUP + h]
        m = jnp.maximum(jnp.max(sf, axis=-1, keepdims=True), sink)
        e = jnp.exp(sf - m)
        sink_terms.append(jnp.exp(sink - m))
        p_s[rows, :BLOCK] = jnp.where(causal, 0.0, e).astype(BF16)
        p_s[rows, BLOCK:] = jnp.where(causal, e, 0.0).astype(BF16)
    return sink_terms


def _attn_values(j, g, sink_terms, p_s, kv_s, sg_s, y_ref):
    col = lax.broadcasted_iota(jnp.int32, (BLOCK, BLOCK), 1)
    lo_half = col < ATT_HEAD_DIM
    rows = slice(j * BLOCK, (j + 1) * BLOCK)
    vcat = kv_s[j * BLOCK:(j + 2) * BLOCK, (2 + g) * LANES:(3 + g) * LANES]
    o = _dot(p_s[...], vcat)
    for pr in range(ATT_GROUP // 2):
        he, ho = 2 * pr, 2 * pr + 1
        oe = o[he * BLOCK:(he + 1) * BLOCK]
        oo = o[ho * BLOCK:(ho + 1) * BLOCK]
        num = jnp.where(lo_half, oe, pltpu.roll(oo, ATT_HEAD_DIM, 1))
        den = jnp.where(lo_half, pltpu.roll(oe, ATT_HEAD_DIM, 1), oo) \
            + jnp.where(lo_half, sink_terms[he], sink_terms[ho])
        c0 = (g * (ATT_GROUP // 2) + pr) * LANES
        y = num * (1.0 / den) * sg_s[rows, c0:c0 + LANES].astype(F32)
        y_ref[rows, c0:c0 + LANES] = y.astype(BF16)


def _gla_independent(c, qd_s, ki_s, kt_s, gv_s):
    sl = slice(c * GLA_CHUNK, (c + 1) * GLA_CHUNK)
    out = []
    for h in range(GLA_HEADS):
        kc = slice(h * GLA_K_DIM, (h + 1) * GLA_K_DIM)
        vc = slice(h * GLA_V_DIM, (h + 1) * GLA_V_DIM)
        out.append((_dot_nt(qd_s[sl, kc], ki_s[sl, kc]), _dot_tn(kt_s[sl, kc], gv_s[sl, vc])))
    return out


def _gla_dependent(c, indep, qd_s, gv_s, sg_s, dk_s, gn_ref, st_s, y_ref):
    row = lax.broadcasted_iota(jnp.int32, (GLA_CHUNK, GLA_CHUNK), 0)
    col = lax.broadcasted_iota(jnp.int32, (GLA_CHUNK, GLA_CHUNK), 1)
    tril = col <= row
    gn = gn_ref[...]
    sl = slice(c * GLA_CHUNK, (c + 1) * GLA_CHUNK)
    for h in range(GLA_HEADS):
        kc = slice(h * GLA_K_DIM, (h + 1) * GLA_K_DIM)
        vc = slice(h * GLA_V_DIM, (h + 1) * GLA_V_DIM)
        oc = slice(ATT_WIDTH + h * GLA_V_DIM, ATT_WIDTH + (h + 1) * GLA_V_DIM)
        scores, kv_inc = indep[h]
        st = st_s[h]
        a = jnp.where(tril, scores, 0.0).astype(BF16)
        o = _dot(jnp.concatenate([qd_s[sl, kc], a], axis=1),
                 jnp.concatenate([st.astype(BF16), gv_s[sl, vc]], axis=0))
        st_s[h] = st * dk_s[kc, c:c + 1] + kv_inc
        y = _rms(o, gn) * sg_s[sl, oc].astype(F32)
        y_ref[sl, oc] = y.astype(BF16)


def _mixer_kernel(sinks_ref, x_ref, pos_ref, nmix_ref, w_ref, wup_ref, bg_ref, rope_ref, ltri_ref, gn_ref,
                  y_ref, u_s, trig_s, glow_s, lg_s, b_s, qstack_s, score_s, p_s,
                  q_s, kv_s, qd_s, ki_s, kt_s, gv_s, sg_s, dk_s, st_s, *,
                  tiles_per_seq):
    i = pl.program_id(0)
    tm = x_ref.shape[0]
    first = lax.rem(i - 1, tiles_per_seq) == 0

    @pl.when(i == 0)
    def _():
        for s in (q_s, kv_s, qd_s, ki_s, kt_s, gv_s, sg_s, dk_s):
            s[1] = jnp.zeros(s.shape[1:], s.dtype)

    @pl.when(first | (i == 0))
    def _():
        st_s[...] = jnp.zeros_like(st_s)

    def step(wr):
        rd = 1 - wr
        inproj = _inproj_pieces(
            x_ref, pos_ref, nmix_ref, w_ref, wup_ref, bg_ref, rope_ref, ltri_ref, u_s, trig_s, glow_s, lg_s, b_s,
            q_s.at[wr], kv_s.at[wr], qd_s.at[wr], ki_s.at[wr], kt_s.at[wr], gv_s.at[wr], sg_s.at[wr],
            dk_s.at[wr])
        q_r, kv_r, sg_r, gv_r, qd_r = q_s.at[rd], kv_s.at[rd], sg_s.at[rd], gv_s.at[rd], qd_s.at[rd]
        first_bias = jnp.where(first, -jnp.inf, 0.0).astype(F32)

        inproj(("norm",))
        for r, (long_group, short_group) in enumerate(_INPROJ_ROUNDS):
            j, g = divmod(r, ATT_KV_HEADS)
            _attn_scores(j, g, q_r, kv_r, qstack_s, score_s)
            sink_terms = _attn_softmax(g, first_bias if j == 0 else None, sinks_ref, score_s, p_s)
            gla_indep = _gla_independent(r, qd_r, ki_s.at[rd], kt_s.at[rd], gv_r)
            for spec in long_group:
                inproj(spec)
            _attn_values(j, g, sink_terms, p_s, kv_r, sg_r, y_ref)
            _gla_dependent(r, gla_indep, qd_r, gv_r, sg_r, dk_s.at[rd], gn_ref, st_s, y_ref)
            for spec in short_group:
                inproj(spec)
        kv_s[wr, 0:BLOCK, :] = kv_s[rd, tm:tm + BLOCK, :]

    assert len(_INPROJ_ROUNDS) == (tm // BLOCK) * ATT_KV_HEADS == tm // GLA_CHUNK
    parity = lax.rem(i, 2)
    for wr in range(2):
        pl.when(parity == wr)(functools.partial(step, wr))


def _wprep_source_row(c):
    n_head = C_GL // W_BLK
    tail_src = C_GL + GLA_GATE_RANK
    return jnp.where(c <= n_head, c * W_BLK, tail_src + (c - n_head - 1) * W_BLK)


def _wprep_kernel(*refs):
    *wt_refs, o_ref = refs
    for k, wt_ref in enumerate(wt_refs):
        c = pl.program_id(0) * WPREP_BLKS + k
        feat = _wprep_source_row(c) + lax.broadcasted_iota(jnp.int32, (W_BLK, 1), 0)
        scale = jnp.where(feat < ATT_WIDTH, ATT_HEAD_DIM ** -0.5, 1.0)
        padding = (c == C_GL // W_BLK) & (feat >= C_GL + GLA_GATE_RANK)
        o_ref[:, k * W_BLK:(k + 1) * W_BLK] = jnp.where(padding, 0.0, wt_ref[...] * scale).T.astype(BF16)


def _out_kernel(x_ref, y_ref, p_ref, wo_ref, pn_ref, wpg_ref, wpp_ref, fn_ref, o_ref, wo_s, wpg_s, wpp_s):
    @pl.when(pl.program_id(0) == 0)
    def _():
        wo_s[...] = wo_ref[...].astype(BF16)
        wpg_s[...] = wpg_ref[...].astype(BF16)
        wpp_s[...] = wpp_ref[...].astype(BF16)

    tm = x_ref.shape[0]
    halves = [slice(k * (tm // 2), (k + 1) * (tm // 2)) for k in range(2)]
    hs = [x_ref[r, :] + _dot(y_ref[r, :], wo_s[...]) for r in halves]
    ples = [_dot(p_ref[r, :].astype(BF16), wpp_s[...]) for r in halves]
    gates = [_dot(_rms(h, pn_ref[...]).astype(BF16), wpg_s[...]) for h in hs]
    for r, h, gate, ple in zip(halves, hs, gates, ples):
        o_ref[r, :] = _rms(h + _sigmoid(gate) * ple, fn_ref[...])


def _const_spec(shape):
    nd = len(shape)
    return pl.BlockSpec(shape, lambda *_: (0,) * nd, pipeline_mode=pl.Buffered(1))


def _rope_freqs():
    assert ROPE_HALF == SUBLANES
    inv_freq = ROPE_THETA ** (-jnp.arange(0, ROPE_DIM, 2, dtype=F32) / ROPE_DIM)
    return jnp.broadcast_to(inv_freq[:, None], (ROPE_HALF, LANES))


def _chunk_tril(n):
    i = np.arange(n)
    same = (i[:, None] // GLA_CHUNK) == (i[None, :] // GLA_CHUNK)
    return (same & (i[None, :] <= i[:, None])).astype(np.float32)


def _layer(h, p, pos, norm_mix, w_in, sinks, w_gate_up, b_gate, gla_norm, w_out, ple_norm,
           w_ple_gate, w_ple_proj, out_norm, batch, seq):
    n = batch * seq
    assert w_in.shape == (D_MODEL, D_IN_PROJ)
    w_t = jnp.swapaxes(w_in, 0, 1)
    src_spec = lambda k: pl.BlockSpec(
        (pl.Element(W_BLK), pl.Element(D_MODEL)),
        lambda c: (pl.multiple_of(_wprep_source_row(c * WPREP_BLKS + k), SUBLANES), 0))
    w_r = pl.pallas_call(
        _wprep_kernel,
        grid=(C_END // (WPREP_BLKS * W_BLK),),
        in_specs=[src_spec(k) for k in range(WPREP_BLKS)],
        out_specs=pl.BlockSpec((D_MODEL, WPREP_BLKS * W_BLK), lambda c: (0, c)),
        out_shape=jax.ShapeDtypeStruct((D_MODEL, C_END), BF16),
        compiler_params=pltpu.CompilerParams(dimension_semantics=("parallel",)),
        name="wprep",
    )(*([w_t] * WPREP_BLKS))
    wup = jnp.concatenate([w_gate_up, jnp.zeros((LANES - GLA_GATE_RANK, GLA_QK_WIDTH), w_gate_up.dtype)],
                          axis=0).astype(BF16)
    rope_freqs = _rope_freqs()
    ltri = jnp.asarray(_chunk_tril(CUM_BLK), dtype=BF16)

    params = lambda sem: pltpu.CompilerParams(dimension_semantics=sem, vmem_limit_bytes=VMEM_LIMIT)

    nt = n // TM_MIX
    in_tile = lambda i: (jnp.minimum(i, nt - 1), 0)
    out_tile = lambda i: (jnp.maximum(i - 1, 0), 0)
    two = lambda *shape: (2,) + shape
    y = pl.pallas_call(
        functools.partial(_mixer_kernel, tiles_per_seq=seq // TM_MIX),
        grid=(nt + 1,),
        in_specs=[
            pl.BlockSpec(memory_space=pltpu.SMEM),
            pl.BlockSpec((TM_MIX, D_MODEL), in_tile),
            pl.BlockSpec((1, TM_MIX // LANES, LANES), lambda i: (jnp.minimum(i, nt - 1), 0, 0)),
            _const_spec((1, D_MODEL)),
            _const_spec((D_MODEL, C_END)),
            _const_spec((LANES, GLA_QK_WIDTH)),
            _const_spec((1, GLA_QK_WIDTH)),
            _const_spec((SUBLANES, LANES)),
            _const_spec((CUM_BLK, CUM_BLK)),
            _const_spec((1, GLA_V_DIM)),
        ],
        out_specs=pl.BlockSpec((TM_MIX, ATT_WIDTH + GLA_WIDTH), out_tile),
        out_shape=jax.ShapeDtypeStruct((n, ATT_WIDTH + GLA_WIDTH), BF16),
        scratch_shapes=[
            pltpu.VMEM((TM_MIX, D_MODEL), BF16),
            pltpu.VMEM((3, TM_MIX, LANES), F32),
            pltpu.VMEM((TM_MIX, LANES), BF16),
            pltpu.VMEM((2, TM_MIX, GLA_QK_WIDTH), BF16),
            pltpu.VMEM((2, TM_MIX, GLA_QK_WIDTH), F32),
            pltpu.VMEM((ATT_GROUP * BLOCK, LANES), BF16),
            pltpu.VMEM((ATT_GROUP * BLOCK, 2 * BLOCK), F32),
            pltpu.VMEM((ATT_GROUP * BLOCK, 2 * BLOCK), BF16),
            pltpu.VMEM(two(TM_MIX, ATT_WIDTH), BF16),
            pltpu.VMEM(two(TM_MIX + BLOCK, 4 * ATT_KV_WIDTH), BF16),
            pltpu.VMEM(two(TM_MIX, GLA_QK_WIDTH), BF16),
            pltpu.VMEM(two(TM_MIX, GLA_QK_WIDTH), BF16),
            pltpu.VMEM(two(TM_MIX, GLA_QK_WIDTH), BF16),
            pltpu.VMEM(two(TM_MIX, GLA_WIDTH), BF16),
            pltpu.VMEM(two(TM_MIX, ATT_WIDTH + GLA_WIDTH), BF16),
            pltpu.VMEM(two(GLA_QK_WIDTH, LANES), F32),
            pltpu.VMEM((GLA_HEADS, GLA_K_DIM, GLA_V_DIM), F32),
        ],
        compiler_params=params(("arbitrary",)),
        name="mixer",
    )(sinks.astype(F32), h, pos.reshape(nt, TM_MIX // LANES, LANES), norm_mix.reshape(1, D_MODEL), w_r, wup,
      b_gate.reshape(1, GLA_QK_WIDTH), rope_freqs, ltri, gla_norm.reshape(1, GLA_V_DIM))

    row = lambda i: (i, 0)
    return pl.pallas_call(
        _out_kernel,
        grid=(n // TM_OUT,),
        in_specs=[
            pl.BlockSpec((TM_OUT, D_MODEL), row),
            pl.BlockSpec((TM_OUT, ATT_WIDTH + GLA_WIDTH), row),
            pl.BlockSpec((TM_OUT, D_PLE), row),
            _const_spec((ATT_WIDTH + GLA_WIDTH, D_MODEL)),
            _const_spec((1, D_MODEL)),
            _const_spec((D_MODEL, D_MODEL)),
            _const_spec((D_PLE, D_MODEL)),
            _const_spec((1, D_MODEL)),
        ],
        out_specs=pl.BlockSpec((TM_OUT, D_MODEL), row),
        out_shape=jax.ShapeDtypeStruct((n, D_MODEL), F32),
        scratch_shapes=[
            pltpu.VMEM((ATT_WIDTH + GLA_WIDTH, D_MODEL), BF16),
            pltpu.VMEM((D_MODEL, D_MODEL), BF16),
            pltpu.VMEM((D_PLE, D_MODEL), BF16),
        ],
        compiler_params=params(("arbitrary",)),
        name="outproj",
    )(h, y, p, w_out, ple_norm.reshape(1, D_MODEL), w_ple_gate, w_ple_proj, out_norm.reshape(1, D_MODEL))


def kernel(x, p, positions, norm_mix, w_in, attn_sinks, w_gate_up, b_gate, gla_norm, w_out, ple_norm,
           w_ple_gate, w_ple_proj, final_norm):
    batch, seq, _ = x.shape
    depth = w_in.shape[0]
    assert depth == 1, "the final RMSNorm is fused into the single layer's output kernel"
    n = batch * seq
    layer = lambda a: a.reshape(a.shape[1:])
    out = _layer(x.reshape(n, D_MODEL), p.reshape(n, D_PLE), positions.reshape(n), layer(norm_mix), layer(w_in),
                 layer(attn_sinks), layer(w_gate_up), layer(b_gate), layer(gla_norm), layer(w_out),
                 layer(ple_norm), layer(w_ple_gate), layer(w_ple_proj), final_norm, batch, seq)
    return out.reshape(batch, seq, D_MODEL)
```

```python
import functools

import jax
import jax.numpy as jnp
import numpy as np
from jax import lax
from jax.experimental import pallas as pl
from jax.experimental.pallas import tpu as pltpu

D_MODEL = 1024
D_PLE = 256
ATT_HEADS = 16
ATT_KV_HEADS = 2
ATT_HEAD_DIM = 64
ATT_WIDTH = ATT_HEADS * ATT_HEAD_DIM
ATT_KV_WIDTH = ATT_KV_HEADS * ATT_HEAD_DIM
ATT_GROUP = ATT_HEADS // ATT_KV_HEADS
BLOCK = 128
ROPE_DIM = ATT_HEAD_DIM // 4
ROPE_HALF = ROPE_DIM // 2
ROPE_THETA = 500000.0
GLA_HEADS = 4
GLA_V_DIM = 256
GLA_K_DIM = 128
GLA_WIDTH = GLA_HEADS * GLA_V_DIM
GLA_QK_WIDTH = GLA_HEADS * GLA_K_DIM
GLA_GATE_RANK = 16
GLA_GATE_NORMALIZER = 16.0
GLA_CHUNK = 64
EPS = 1e-6

LANES = 128
SUBLANES = 8
VMEM_LIMIT = 56 * 1024 * 1024

W_BLK = 256
WPREP_BLKS = 2

C_Q = 0
C_KV = C_Q + ATT_WIDTH
C_GQ = C_KV + 2 * ATT_KV_WIDTH
C_GK = C_GQ + GLA_QK_WIDTH
C_GV = C_GK + GLA_QK_WIDTH
C_GL = C_GV + GLA_WIDTH
C_ZA = C_GL + W_BLK
C_ZG = C_ZA + ATT_WIDTH
C_END = C_ZG + GLA_WIDTH
D_IN_PROJ = C_END - W_BLK + GLA_GATE_RANK

TM_MIX = 512
TM_OUT = 512
CUM_BLK = 256

BF16 = jnp.bfloat16
F32 = jnp.float32


def _dot(a, b):
    return jnp.dot(a, b, preferred_element_type=F32)


def _dot_nt(a, b):
    return lax.dot_general(a, b, (((1,), (1,)), ((), ())), preferred_element_type=F32)


def _dot_tn(a, b):
    return lax.dot_general(a, b, (((0,), (0,)), ((), ())), preferred_element_type=F32)


def _rms(x, g):
    return x * lax.rsqrt(jnp.mean(x * x, axis=-1, keepdims=True) + EPS) * g


def _sigmoid(z):
    return 0.5 * (1.0 + jnp.tanh(0.5 * z))


def _inproj_pieces(x_ref, pos_ref, nmix_ref, w_ref, wup_ref, bg_ref, rope_ref, ltri_ref,
                   u_s, trig_s, glow_s, lg_s, b_s, q_s, kv_s, qd_s, ki_s, kt_s, gv_s, sg_s, dk_s):
    tm = x_ref.shape[0]

    def proj(c0, width):
        return _dot(u_s[...], w_ref[:, c0:c0 + width])

    def norm():
        u_s[...] = _rms(x_ref[...], nmix_ref[...]).astype(BF16)
        pos = pos_ref[0].astype(F32)
        ones = jnp.ones((ATT_HEAD_DIM - ROPE_DIM, LANES), F32)
        zeros = jnp.zeros((ATT_HEAD_DIM - ROPE_DIM, LANES), F32)
        zero8 = jnp.zeros((ROPE_HALF, LANES), F32)
        for r in range(tm // LANES):
            ang = rope_ref[...] * pos[r:r + 1, :]
            c8 = jnp.cos(ang)
            s8 = jnp.sin(ang)
            rows = slice(r * LANES, (r + 1) * LANES)
            trig_s[0, rows, :] = jnp.concatenate([c8, c8, ones] * 2, axis=0).T
            trig_s[1, rows, :] = jnp.concatenate([-s8, zero8, zeros] * 2, axis=0).T
            trig_s[2, rows, :] = jnp.concatenate([zero8, s8, zeros] * 2, axis=0).T

    def rope(z):
        up = pltpu.roll(z, LANES - ROPE_HALF, 1)
        dn = pltpu.roll(z, ROPE_HALF, 1)
        return z * trig_s[0] + up * trig_s[1] + dn * trig_s[2]

    def q_piece(lo, width):
        z = proj(C_Q + lo, width)
        for l in range(0, width, LANES):
            q_s[:, lo + l:lo + l + LANES] = rope(z[:, l:l + LANES]).astype(BF16)

    def kv_piece(lo, width):
        assert (lo, width) == (0, 2 * ATT_KV_WIDTH)
        z = proj(C_KV, width)
        lo_half = lax.broadcasted_iota(jnp.int32, (tm, LANES), 1) < ATT_HEAD_DIM
        k = rope(z[:, :LANES])
        k_sw = pltpu.roll(k, ATT_HEAD_DIM, 1)
        v = z[:, LANES:]
        v_sw = pltpu.roll(v, ATT_HEAD_DIM, 1)
        kv_s[BLOCK:, 0 * LANES:1 * LANES] = jnp.where(lo_half, k, k_sw).astype(BF16)
        kv_s[BLOCK:, 1 * LANES:2 * LANES] = jnp.where(lo_half, k_sw, k).astype(BF16)
        kv_s[BLOCK:, 2 * LANES:3 * LANES] = jnp.where(lo_half, v, 1.0).astype(BF16)
        kv_s[BLOCK:, 3 * LANES:4 * LANES] = jnp.where(lo_half, v_sw, 1.0).astype(BF16)

    def gv_piece(lo, width):
        gv_s[:, lo:lo + width] = proj(C_GV + lo, width).astype(BF16)

    def sg_piece(lo, width):
        z = proj(C_ZA + lo, width)
        sg_s[:, lo:lo + width] = (z * _sigmoid(z)).astype(BF16)

    def gate_low():
        glow_s[...] = proj(C_GL, LANES).astype(BF16)

    def gate_logsig():
        logit = _dot(glow_s[...], wup_ref[...]) + bg_ref[...]
        lg = (jnp.minimum(logit, 0.0) - jnp.log(1.0 + jnp.exp(-jnp.abs(logit)))) * (1.0 / GLA_GATE_NORMALIZER)
        lg_hi = lg.astype(BF16)
        lg_s[0] = lg_hi
        lg_s[1] = (lg - lg_hi.astype(F32)).astype(BF16)

    def gate_cumsum():
        ltri = ltri_ref[...]
        nb = CUM_BLK // GLA_CHUNK
        chunk_totals = []
        for r in range(0, tm, CUM_BLK):
            b = _dot(ltri, lg_s[0, r:r + CUM_BLK, :]) + _dot(ltri, lg_s[1, r:r + CUM_BLK, :])
            b3 = b.reshape(nb, GLA_CHUNK, GLA_QK_WIDTH)
            b_last = b3[:, GLA_CHUNK - 1:GLA_CHUNK, :]
            b_s[0, r:r + CUM_BLK, :] = b
            b_s[1, r:r + CUM_BLK, :] = (b_last - b3).reshape(CUM_BLK, GLA_QK_WIDTH)
            chunk_totals.append(b_last.reshape(nb, GLA_QK_WIDTH))
        pad = jnp.zeros((LANES - tm // GLA_CHUNK, GLA_QK_WIDTH), F32)
        decay = jnp.exp(jnp.concatenate(chunk_totals + [pad], axis=0))
        for h in range(GLA_HEADS):
            kc = slice(h * GLA_K_DIM, (h + 1) * GLA_K_DIM)
            dk_s[kc, :] = decay[:, kc].T

    def gq_piece(lo, width):
        z = proj(C_GQ + lo, width) * (GLA_K_DIM ** -0.5)
        qd_s[:, lo:lo + width] = (z * jnp.exp(b_s[0, :, lo:lo + width])).astype(BF16)

    def gk_piece(lo, width):
        z = proj(C_GK + lo, width)
        ki_s[:, lo:lo + width] = (z * jnp.exp(-b_s[0, :, lo:lo + width])).astype(BF16)
        kt_s[:, lo:lo + width] = (z * jnp.exp(b_s[1, :, lo:lo + width])).astype(BF16)

    column_pieces = {"q": q_piece, "kv": kv_piece, "gv": gv_piece, "sg": sg_piece, "gq": gq_piece, "gk": gk_piece}
    single_pieces = {"norm": norm, "gate_low": gate_low, "gate_logsig": gate_logsig, "gate_cumsum": gate_cumsum}

    def run(spec):
        if spec[0] in single_pieces:
            single_pieces[spec[0]]()
        else:
            column_pieces[spec[0]](spec[1], spec[2])
    return run


_INPROJ_ROUNDS = (
    ((("q", 0, 512),), (("q", 512, 256),)),
    ((("q", 768, 256), ("kv", 0, 256)), (("gate_low",),)),
    ((("gv", 0, 512),), (("gate_logsig",),)),
    ((("gv", 512, 512),), (("gate_cumsum",),)),
    ((("sg", 0, 512),), (("gq", 0, 256),)),
    ((("sg", 512, 512),), (("gq", 256, 256),)),
    ((("sg", 1024, 512),), (("gk", 0, 256),)),
    ((("sg", 1536, 512),), (("gk", 256, 256),)),
)


def _attn_scores(j, g, q_s, kv_s, qstack_s, score_s):
    col = lax.broadcasted_iota(jnp.int32, (BLOCK, BLOCK), 1)
    lo_half = col < ATT_HEAD_DIM
    zero = jnp.zeros((), BF16)
    kcat = kv_s[j * BLOCK:(j + 2) * BLOCK, g * LANES:(g + 1) * LANES]
    for pr in range(ATT_GROUP // 2):
        c0 = (g * (ATT_GROUP // 2) + pr) * LANES
        qp = q_s[j * BLOCK:(j + 1) * BLOCK, c0:c0 + LANES]
        qstack_s[(2 * pr) * BLOCK:(2 * pr + 1) * BLOCK, :] = jnp.where(lo_half, qp, zero)
        qstack_s[(2 * pr + 1) * BLOCK:(2 * pr + 2) * BLOCK, :] = jnp.where(lo_half, zero, qp)
    score_s[...] = _dot_nt(qstack_s[...], kcat)


def _attn_softmax(g, prev_bias, sinks_ref, score_s, p_s):
    row = lax.broadcasted_iota(jnp.int32, (BLOCK, BLOCK), 0)
    col = lax.broadcasted_iota(jnp.int32, (BLOCK, BLOCK), 1)
    causal = col <= row
    sink_terms = []
    for h in range(ATT_GROUP):
        rows = slice(h * BLOCK, (h + 1) * BLOCK)
        s_prev = score_s[rows, :BLOCK]
        if prev_bias is not None:
            s_prev = s_prev + prev_bias
        sf = jnp.where(causal, score_s[rows, BLOCK:], s_prev)
        sink = sinks_ref[g * ATT_GROUP + h]
        m = jnp.maximum(jnp.max(sf, axis=-1, keepdims=True), sink)
        e = jnp.exp(sf - m)
        sink_terms.append(jnp.exp(sink - m))
        p_s[rows, :BLOCK] = jnp.where(causal, 0.0, e).astype(BF16)
        p_s[rows, BLOCK:] = jnp.where(causal, e, 0.0).astype(BF16)
    return sink_terms


def _attn_values(j, g, sink_terms, p_s, kv_s, sg_s, y_ref):
    col = lax.broadcasted_iota(jnp.int32, (BLOCK, BLOCK), 1)
    lo_half = col < ATT_HEAD_DIM
    rows = slice(j * BLOCK, (j + 1) * BLOCK)
    vcat = kv_s[j * BLOCK:(j + 2) * BLOCK, (2 + g) * LANES:(3 + g) * LANES]
    o = _dot(p_s[...], vcat)
    for pr in range(ATT_GROUP // 2):
        he, ho = 2 * pr, 2 * pr + 1
        oe = o[he * BLOCK:(he + 1) * BLOCK]
        oo = o[ho * BLOCK:(ho + 1) * BLOCK]
        num = jnp.where(lo_half, oe, pltpu.roll(oo, ATT_HEAD_DIM, 1))
        den = jnp.where(lo_half, pltpu.roll(oe, ATT_HEAD_DIM, 1), oo) \
            + jnp.where(lo_half, sink_terms[he], sink_terms[ho])
        c0 = (g * (ATT_GROUP // 2) + pr) * LANES
        y = num * (1.0 / den) * sg_s[rows, c0:c0 + LANES].astype(F32)
        y_ref[rows, c0:c0 + LANES] = y.astype(BF16)


def _gla_independent(c, qd_s, ki_s, kt_s, gv_s):
    sl = slice(c * GLA_CHUNK, (c + 1) * GLA_CHUNK)
    out = []
    for h in range(GLA_HEADS):
        kc = slice(h * GLA_K_DIM, (h + 1) * GLA_K_DIM)
        vc = slice(h * GLA_V_DIM, (h + 1) * GLA_V_DIM)
        out.append((_dot_nt(qd_s[sl, kc], ki_s[sl, kc]), _dot_tn(kt_s[sl, kc], gv_s[sl, vc])))
    return out


def _gla_dependent(c, indep, qd_s, gv_s, sg_s, dk_s, gn_ref, st_s, y_ref):
    row = lax.broadcasted_iota(jnp.int32, (GLA_CHUNK, GLA_CHUNK), 0)
    col = lax.broadcasted_iota(jnp.int32, (GLA_CHUNK, GLA_CHUNK), 1)
    tril = col <= row
    gn = gn_ref[...]
    sl = slice(c * GLA_CHUNK, (c + 1) * GLA_CHUNK)
    for h in range(GLA_HEADS):
        kc = slice(h * GLA_K_DIM, (h + 1) * GLA_K_DIM)
        vc = slice(h * GLA_V_DIM, (h + 1) * GLA_V_DIM)
        oc = slice(ATT_WIDTH + h * GLA_V_DIM, ATT_WIDTH + (h + 1) * GLA_V_DIM)
        scores, kv_inc = indep[h]
        st = st_s[h]
        a = jnp.where(tril, scores, 0.0).astype(BF16)
        o = _dot(jnp.concatenate([qd_s[sl, kc], a], axis=1),
                 jnp.concatenate([st.astype(BF16), gv_s[sl, vc]], axis=0))
        st_s[h] = st * dk_s[kc, c:c + 1] + kv_inc
        y = _rms(o, gn) * sg_s[sl, oc].astype(F32)
        y_ref[sl, oc] = y.astype(BF16)


def _mixer_kernel(sinks_ref, x_ref, pos_ref, nmix_ref, w_ref, wup_ref, bg_ref, rope_ref, ltri_ref, gn_ref,
                  y_ref, u_s, trig_s, glow_s, lg_s, b_s, qstack_s, score_s, p_s,
                  q_s, kv_s, qd_s, ki_s, kt_s, gv_s, sg_s, dk_s, st_s, *,
                  tiles_per_seq, n_tiles):
    i = pl.program_id(0)
    tm = x_ref.shape[0]
    first = lax.rem(i - 1, tiles_per_seq) == 0

    @pl.when(first)
    def _():
        st_s[...] = jnp.zeros_like(st_s)

    def step(wr, do_inproj, do_mix):
        rd = 1 - wr
        inproj = _inproj_pieces(
            x_ref, pos_ref, nmix_ref, w_ref, wup_ref, bg_ref, rope_ref, ltri_ref, u_s, trig_s, glow_s, lg_s, b_s,
            q_s.at[wr], kv_s.at[wr], qd_s.at[wr], ki_s.at[wr], kt_s.at[wr], gv_s.at[wr], sg_s.at[wr],
            dk_s.at[wr])
        q_r, kv_r, sg_r, gv_r, qd_r = q_s.at[rd], kv_s.at[rd], sg_s.at[rd], gv_s.at[rd], qd_s.at[rd]
        first_bias = jnp.where(first, -jnp.inf, 0.0).astype(F32)

        if do_inproj:
            inproj(("norm",))
        for r, (long_group, short_group) in enumerate(_INPROJ_ROUNDS):
            j, g = divmod(r, ATT_KV_HEADS)
            if do_mix:
                _attn_scores(j, g, q_r, kv_r, qstack_s, score_s)
                sink_terms = _attn_softmax(g, first_bias if j == 0 else None, sinks_ref, score_s, p_s)
                gla_indep = _gla_independent(r, qd_r, ki_s.at[rd], kt_s.at[rd], gv_r)
            if do_inproj:
                for spec in long_group:
                    inproj(spec)
            if do_mix:
                _attn_values(j, g, sink_terms, p_s, kv_r, sg_r, y_ref)
                _gla_dependent(r, gla_indep, qd_r, gv_r, sg_r, dk_s.at[rd], gn_ref, st_s, y_ref)
            if do_inproj:
                for spec in short_group:
                    inproj(spec)
        if do_inproj:
            kv_s[wr, 0:BLOCK, :] = (kv_s[rd, tm:tm + BLOCK, :] if do_mix
                                    else jnp.zeros((BLOCK, kv_s.shape[2]), kv_s.dtype))

    assert len(_INPROJ_ROUNDS) == (tm // BLOCK) * ATT_KV_HEADS == tm // GLA_CHUNK
    is_first, is_last = i == 0, i == n_tiles
    pl.when(is_first)(functools.partial(step, 0, True, False))
    pl.when(is_last)(functools.partial(step, n_tiles % 2, False, True))
    for wr in range(2):
        pl.when((lax.rem(i, 2) == wr) & ~is_first & ~is_last)(functools.partial(step, wr, True, True))


def _wprep_source_row(c):
    n_head = C_GL // W_BLK
    tail_src = C_GL + GLA_GATE_RANK
    return jnp.where(c <= n_head, c * W_BLK, tail_src + (c - n_head - 1) * W_BLK)


def _wprep_kernel(*refs):
    *wt_refs, o_ref = refs
    for k, wt_ref in enumerate(wt_refs):
        c = pl.program_id(0) * WPREP_BLKS + k
        feat = _wprep_source_row(c) + lax.broadcasted_iota(jnp.int32, (W_BLK, 1), 0)
        scale = jnp.where(feat < ATT_WIDTH, ATT_HEAD_DIM ** -0.5, 1.0)
        padding = (c == C_GL // W_BLK) & (feat >= C_GL + GLA_GATE_RANK)
        o_ref[:, k * W_BLK:(k + 1) * W_BLK] = jnp.where(padding, 0.0, wt_ref[...] * scale).T.astype(BF16)


def _out_kernel(x_ref, y_ref, p_ref, wo_ref, pn_ref, wpg_ref, wpp_ref, fn_ref, o_ref, wo_s, wpg_s, wpp_s):
    @pl.when(pl.program_id(0) == 0)
    def _():
        wo_s[...] = wo_ref[...].astype(BF16)
        wpg_s[...] = wpg_ref[...].astype(BF16)
        wpp_s[...] = wpp_ref[...].astype(BF16)

    tm = x_ref.shape[0]
    halves = [slice(k * (tm // 2), (k + 1) * (tm // 2)) for k in range(2)]
    hs = [x_ref[r, :] + _dot(y_ref[r, :], wo_s[...]) for r in halves]
    ples = [_dot(p_ref[r, :].astype(BF16), wpp_s[...]) for r in halves]
    gates = [_dot(_rms(h, pn_ref[...]).astype(BF16), wpg_s[...]) for h in hs]
    for r, h, gate, ple in zip(halves, hs, gates, ples):
        o_ref[r, :] = _rms(h + _sigmoid(gate) * ple, fn_ref[...])


def _const_spec(shape):
    nd = len(shape)
    return pl.BlockSpec(shape, lambda *_: (0,) * nd, pipeline_mode=pl.Buffered(1))


def _rope_freqs():
    assert ROPE_HALF == SUBLANES
    inv_freq = ROPE_THETA ** (-jnp.arange(0, ROPE_DIM, 2, dtype=F32) / ROPE_DIM)
    return jnp.broadcast_to(inv_freq[:, None], (ROPE_HALF, LANES))


def _chunk_tril(n):
    i = np.arange(n)
    same = (i[:, None] // GLA_CHUNK) == (i[None, :] // GLA_CHUNK)
    return (same & (i[None, :] <= i[:, None])).astype(np.float32)


def _layer(h, p, pos, norm_mix, w_in, sinks, w_gate_up, b_gate, gla_norm, w_out, ple_norm,
           w_ple_gate, w_ple_proj, out_norm, batch, seq):
    n = batch * seq
    assert w_in.shape == (D_MODEL, D_IN_PROJ)
    w_t = jnp.swapaxes(w_in, 0, 1)
    src_spec = lambda k: pl.BlockSpec(
        (pl.Element(W_BLK), pl.Element(D_MODEL)),
        lambda c: (pl.multiple_of(_wprep_source_row(c * WPREP_BLKS + k), SUBLANES), 0))
    w_r = pl.pallas_call(
        _wprep_kernel,
        grid=(C_END // (WPREP_BLKS * W_BLK),),
        in_specs=[src_spec(k) for k in range(WPREP_BLKS)],
        out_specs=pl.BlockSpec((D_MODEL, WPREP_BLKS * W_BLK), lambda c: (0, c)),
        out_shape=jax.ShapeDtypeStruct((D_MODEL, C_END), BF16),
        compiler_params=pltpu.CompilerParams(dimension_semantics=("parallel",)),
        name="wprep",
    )(*([w_t] * WPREP_BLKS))
    wup = jnp.concatenate([w_gate_up, jnp.zeros((LANES - GLA_GATE_RANK, GLA_QK_WIDTH), w_gate_up.dtype)],
                          axis=0).astype(BF16)
    rope_freqs = _rope_freqs()
    ltri = jnp.asarray(_chunk_tril(CUM_BLK), dtype=BF16)

    params = lambda sem: pltpu.CompilerParams(dimension_semantics=sem, vmem_limit_bytes=VMEM_LIMIT)

    nt = n // TM_MIX
    in_tile = lambda i: (jnp.minimum(i, nt - 1), 0)
    out_tile = lambda i: (jnp.maximum(i - 1, 0), 0)
    two = lambda *shape: (2,) + shape
    y = pl.pallas_call(
        functools.partial(_mixer_kernel, tiles_per_seq=seq // TM_MIX, n_tiles=nt),
        grid=(nt + 1,),
        in_specs=[
            pl.BlockSpec(memory_space=pltpu.SMEM),
            pl.BlockSpec((TM_MIX, D_MODEL), in_tile),
            pl.BlockSpec((1, TM_MIX // LANES, LANES), lambda i: (jnp.minimum(i, nt - 1), 0, 0)),
            _const_spec((1, D_MODEL)),
            _const_spec((D_MODEL, C_END)),
            _const_spec((LANES, GLA_QK_WIDTH)),
            _const_spec((1, GLA_QK_WIDTH)),
            _const_spec((SUBLANES, LANES)),
            _const_spec((CUM_BLK, CUM_BLK)),
            _const_spec((1, GLA_V_DIM)),
        ],
        out_specs=pl.BlockSpec((TM_MIX, ATT_WIDTH + GLA_WIDTH), out_tile),
        out_shape=jax.ShapeDtypeStruct((n, ATT_WIDTH + GLA_WIDTH), BF16),
        scratch_shapes=[
            pltpu.VMEM((TM_MIX, D_MODEL), BF16),
            pltpu.VMEM((3, TM_MIX, LANES), F32),
            pltpu.VMEM((TM_MIX, LANES), BF16),
            pltpu.VMEM((2, TM_MIX, GLA_QK_WIDTH), BF16),
            pltpu.VMEM((2, TM_MIX, GLA_QK_WIDTH), F32),
            pltpu.VMEM((ATT_GROUP * BLOCK, LANES), BF16),
            pltpu.VMEM((ATT_GROUP * BLOCK, 2 * BLOCK), F32),
            pltpu.VMEM((ATT_GROUP * BLOCK, 2 * BLOCK), BF16),
            pltpu.VMEM(two(TM_MIX, ATT_WIDTH), BF16),
            pltpu.VMEM(two(TM_MIX + BLOCK, 4 * ATT_KV_WIDTH), BF16),
            pltpu.VMEM(two(TM_MIX, GLA_QK_WIDTH), BF16),
            pltpu.VMEM(two(TM_MIX, GLA_QK_WIDTH), BF16),
            pltpu.VMEM(two(TM_MIX, GLA_QK_WIDTH), BF16),
            pltpu.VMEM(two(TM_MIX, GLA_WIDTH), BF16),
            pltpu.VMEM(two(TM_MIX, ATT_WIDTH + GLA_WIDTH), BF16),
            pltpu.VMEM(two(GLA_QK_WIDTH, LANES), F32),
            pltpu.VMEM((GLA_HEADS, GLA_K_DIM, GLA_V_DIM), F32),
        ],
        compiler_params=params(("arbitrary",)),
        name="mixer",
    )(sinks.astype(F32), h, pos.reshape(nt, TM_MIX // LANES, LANES), norm_mix.reshape(1, D_MODEL), w_r, wup,
      b_gate.reshape(1, GLA_QK_WIDTH), rope_freqs, ltri, gla_norm.reshape(1, GLA_V_DIM))

    row = lambda i: (i, 0)
    return pl.pallas_call(
        _out_kernel,
        grid=(n // TM_OUT,),
        in_specs=[
            pl.BlockSpec((TM_OUT, D_MODEL), row),
            pl.BlockSpec((TM_OUT, ATT_WIDTH + GLA_WIDTH), row),
            pl.BlockSpec((TM_OUT, D_PLE), row),
            _const_spec((ATT_WIDTH + GLA_WIDTH, D_MODEL)),
            _const_spec((1, D_MODEL)),
            _const_spec((D_MODEL, D_MODEL)),
            _const_spec((D_PLE, D_MODEL)),
            _const_spec((1, D_MODEL)),
        ],
        out_specs=pl.BlockSpec((TM_OUT, D_MODEL), row),
        out_shape=jax.ShapeDtypeStruct((n, D_MODEL), F32),
        scratch_shapes=[
            pltpu.VMEM((ATT_WIDTH + GLA_WIDTH, D_MODEL), BF16),
            pltpu.VMEM((D_MODEL, D_MODEL), BF16),
            pltpu.VMEM((D_PLE, D_MODEL), BF16),
        ],
        compiler_params=params(("arbitrary",)),
        name="outproj",
    )(h, y, p, w_out, ple_norm.reshape(1, D_MODEL), w_ple_gate, w_ple_proj, out_norm.reshape(1, D_MODEL))


def kernel(x, p, positions, norm_mix, w_in, attn_sinks, w_gate_up, b_gate, gla_norm, w_out, ple_norm,
           w_ple_gate, w_ple_proj, final_norm):
    batch, seq, _ = x.shape
    depth = w_in.shape[0]
    assert depth == 1, "the final RMSNorm is fused into the single layer's output kernel"
    n = batch * seq
    layer = lambda a: a.reshape(a.shape[1:])
    out = _layer(x.reshape(n, D_MODEL), p.reshape(n, D_PLE), positions.reshape(n), layer(norm_mix), layer(w_in),
                 layer(attn_sinks), layer(w_gate_up), layer(b_gate), layer(gla_norm), layer(w_out),
                 layer(ple_norm), layer(w_ple_gate), layer(w_ple_proj), final_norm, batch, seq)
    return out.reshape(batch, seq, D_MODEL)
```

```python
import functools

import jax
import jax.numpy as jnp
import numpy as np
from jax import lax
from jax.experimental import pallas as pl
from jax.experimental.pallas import tpu as pltpu

D_MODEL = 1024
D_PLE = 256
ATT_HEADS = 16
ATT_KV_HEADS = 2
ATT_HEAD_DIM = 64
ATT_WIDTH = ATT_HEADS * ATT_HEAD_DIM
ATT_KV_WIDTH = ATT_KV_HEADS * ATT_HEAD_DIM
ATT_GROUP = ATT_HEADS // ATT_KV_HEADS
BLOCK = 128
ROPE_DIM = ATT_HEAD_DIM // 4
ROPE_HALF = ROPE_DIM // 2
ROPE_THETA = 500000.0
GLA_HEADS = 4
GLA_V_DIM = 256
GLA_K_DIM = 128
GLA_WIDTH = GLA_HEADS * GLA_V_DIM
GLA_QK_WIDTH = GLA_HEADS * GLA_K_DIM
GLA_GATE_RANK = 16
GLA_GATE_NORMALIZER = 16.0
GLA_CHUNK = 64
EPS = 1e-6

LANES = 128
SUBLANES = 8
VMEM_LIMIT = 56 * 1024 * 1024

W_BLK = 256
WPREP_BLKS = 2

C_Q = 0
C_KV = C_Q + ATT_WIDTH
C_GQ = C_KV + 2 * ATT_KV_WIDTH
C_GK = C_GQ + GLA_QK_WIDTH
C_GV = C_GK + GLA_QK_WIDTH
C_GL = C_GV + GLA_WIDTH
C_ZA = C_GL + W_BLK
C_ZG = C_ZA + ATT_WIDTH
C_END = C_ZG + GLA_WIDTH
D_IN_PROJ = C_END - W_BLK + GLA_GATE_RANK

TM_MIX = 256
TM_OUT = 512
CUM_BLK = 256

BF16 = jnp.bfloat16
F32 = jnp.float32


def _dot(a, b):
    return jnp.dot(a, b, preferred_element_type=F32)


def _dot_nt(a, b):
    return lax.dot_general(a, b, (((1,), (1,)), ((), ())), preferred_element_type=F32)


def _dot_tn(a, b):
    return lax.dot_general(a, b, (((0,), (0,)), ((), ())), preferred_element_type=F32)


def _rms(x, g):
    return x * lax.rsqrt(jnp.mean(x * x, axis=-1, keepdims=True) + EPS) * g


def _sigmoid(z):
    return 0.5 * (1.0 + jnp.tanh(0.5 * z))


def _inproj_pieces(x_ref, pos_ref, nmix_ref, w_ref, wup_ref, bg_ref, rope_ref, ltri_ref,
                   u_s, trig_s, glow_s, lg_s, b_s, q_s, kv_s, qd_s, ki_s, kt_s, gv_s, sg_s, dk_s):
    tm = x_ref.shape[0]

    def proj(c0, width):
        return _dot(u_s[...], w_ref[:, c0:c0 + width])

    def norm():
        u_s[...] = _rms(x_ref[...], nmix_ref[...]).astype(BF16)
        pos = pos_ref[0].astype(F32)
        ones = jnp.ones((ATT_HEAD_DIM - ROPE_DIM, LANES), F32)
        zeros = jnp.zeros((ATT_HEAD_DIM - ROPE_DIM, LANES), F32)
        zero8 = jnp.zeros((ROPE_HALF, LANES), F32)
        for r in range(tm // LANES):
            ang = rope_ref[...] * pos[r:r + 1, :]
            c8 = jnp.cos(ang)
            s8 = jnp.sin(ang)
            rows = slice(r * LANES, (r + 1) * LANES)
            trig_s[0, rows, :] = jnp.concatenate([c8, c8, ones] * 2, axis=0).T
            trig_s[1, rows, :] = jnp.concatenate([-s8, zero8, zeros] * 2, axis=0).T
            trig_s[2, rows, :] = jnp.concatenate([zero8, s8, zeros] * 2, axis=0).T

    def rope(z):
        up = pltpu.roll(z, LANES - ROPE_HALF, 1)
        dn = pltpu.roll(z, ROPE_HALF, 1)
        return z * trig_s[0] + up * trig_s[1] + dn * trig_s[2]

    def q_piece(lo, width):
        z = proj(C_Q + lo, width)
        for l in range(0, width, LANES):
            q_s[:, lo + l:lo + l + LANES] = rope(z[:, l:l + LANES]).astype(BF16)

    def kv_piece(lo, width):
        assert (lo, width) == (0, 2 * ATT_KV_WIDTH)
        z = proj(C_KV, width)
        lo_half = lax.broadcasted_iota(jnp.int32, (tm, LANES), 1) < ATT_HEAD_DIM
        k = rope(z[:, :LANES])
        k_sw = pltpu.roll(k, ATT_HEAD_DIM, 1)
        v = z[:, LANES:]
        v_sw = pltpu.roll(v, ATT_HEAD_DIM, 1)
        kv_s[BLOCK:, 0 * LANES:1 * LANES] = jnp.where(lo_half, k, k_sw).astype(BF16)
        kv_s[BLOCK:, 1 * LANES:2 * LANES] = jnp.where(lo_half, k_sw, k).astype(BF16)
        kv_s[BLOCK:, 2 * LANES:3 * LANES] = jnp.where(lo_half, v, 1.0).astype(BF16)
        kv_s[BLOCK:, 3 * LANES:4 * LANES] = jnp.where(lo_half, v_sw, 1.0).astype(BF16)

    def gv_piece(lo, width):
        gv_s[:, lo:lo + width] = proj(C_GV + lo, width).astype(BF16)

    def sg_piece(lo, width):
        z = proj(C_ZA + lo, width)
        sg_s[:, lo:lo + width] = (z * _sigmoid(z)).astype(BF16)

    def gate_low():
        glow_s[...] = proj(C_GL, LANES).astype(BF16)

    def gate_logsig():
        logit = _dot(glow_s[...], wup_ref[...]) + bg_ref[...]
        lg = (jnp.minimum(logit, 0.0) - jnp.log(1.0 + jnp.exp(-jnp.abs(logit)))) * (1.0 / GLA_GATE_NORMALIZER)
        lg_hi = lg.astype(BF16)
        lg_s[0] = lg_hi
        lg_s[1] = (lg - lg_hi.astype(F32)).astype(BF16)

    def gate_cumsum():
        ltri = ltri_ref[...]
        nb = CUM_BLK // GLA_CHUNK
        chunk_totals = []
        for r in range(0, tm, CUM_BLK):
            b = _dot(ltri, lg_s[0, r:r + CUM_BLK, :]) + _dot(ltri, lg_s[1, r:r + CUM_BLK, :])
            b3 = b.reshape(nb, GLA_CHUNK, GLA_QK_WIDTH)
            b_last = b3[:, GLA_CHUNK - 1:GLA_CHUNK, :]
            b_s[0, r:r + CUM_BLK, :] = b
            b_s[1, r:r + CUM_BLK, :] = (b_last - b3).reshape(CUM_BLK, GLA_QK_WIDTH)
            chunk_totals.append(b_last.reshape(nb, GLA_QK_WIDTH))
        pad = jnp.zeros((LANES - tm // GLA_CHUNK, GLA_QK_WIDTH), F32)
        decay = jnp.exp(jnp.concatenate(chunk_totals + [pad], axis=0))
        for h in range(GLA_HEADS):
            kc = slice(h * GLA_K_DIM, (h + 1) * GLA_K_DIM)
            dk_s[kc, :] = decay[:, kc].T

    def gq_piece(lo, width):
        z = proj(C_GQ + lo, width) * (GLA_K_DIM ** -0.5)
        qd_s[:, lo:lo + width] = (z * jnp.exp(b_s[0, :, lo:lo + width])).astype(BF16)

    def gk_piece(lo, width):
        z = proj(C_GK + lo, width)
        ki_s[:, lo:lo + width] = (z * jnp.exp(-b_s[0, :, lo:lo + width])).astype(BF16)
        kt_s[:, lo:lo + width] = (z * jnp.exp(b_s[1, :, lo:lo + width])).astype(BF16)

    column_pieces = {"q": q_piece, "kv": kv_piece, "gv": gv_piece, "sg": sg_piece, "gq": gq_piece, "gk": gk_piece}
    single_pieces = {"norm": norm, "gate_low": gate_low, "gate_logsig": gate_logsig, "gate_cumsum": gate_cumsum}

    def run(spec):
        if spec[0] in single_pieces:
            single_pieces[spec[0]]()
        else:
            column_pieces[spec[0]](spec[1], spec[2])
    return run


_INPROJ_ROUNDS = (
    ((("q", 0, 512), ("q", 512, 512)), (("kv", 0, 256),)),
    ((("gv", 0, 512), ("gate_low",), ("gv", 512, 512)), (("gate_logsig",),)),
    ((("sg", 0, 512), ("sg", 512, 512)), (("gate_cumsum",),)),
    ((("sg", 1024, 512), ("gq", 0, 512)), (("sg", 1536, 512), ("gk", 0, 512))),
)


def _attn_scores(j, g, q_s, kv_s, qstack_s, score_s):
    col = lax.broadcasted_iota(jnp.int32, (BLOCK, BLOCK), 1)
    lo_half = col < ATT_HEAD_DIM
    zero = jnp.zeros((), BF16)
    kcat = kv_s[j * BLOCK:(j + 2) * BLOCK, g * LANES:(g + 1) * LANES]
    for pr in range(ATT_GROUP // 2):
        c0 = (g * (ATT_GROUP // 2) + pr) * LANES
        qp = q_s[j * BLOCK:(j + 1) * BLOCK, c0:c0 + LANES]
        qstack_s[(2 * pr) * BLOCK:(2 * pr + 1) * BLOCK, :] = jnp.where(lo_half, qp, zero)
        qstack_s[(2 * pr + 1) * BLOCK:(2 * pr + 2) * BLOCK, :] = jnp.where(lo_half, zero, qp)
    score_s[...] = _dot_nt(qstack_s[...], kcat)


def _attn_softmax(g, prev_bias, sinks_ref, score_s, p_s):
    row = lax.broadcasted_iota(jnp.int32, (BLOCK, BLOCK), 0)
    col = lax.broadcasted_iota(jnp.int32, (BLOCK, BLOCK), 1)
    causal = col <= row
    sink_terms = []
    for h in range(ATT_GROUP):
        rows = slice(h * BLOCK, (h + 1) * BLOCK)
        s_prev = score_s[rows, :BLOCK]
        if prev_bias is not None:
            s_prev = s_prev + prev_bias
        sf = jnp.where(causal, score_s[rows, BLOCK:], s_prev)
        sink = sinks_ref[g * ATT_GROUP + h]
        m = jnp.maximum(jnp.max(sf, axis=-1, keepdims=True), sink)
        e = jnp.exp(sf - m)
        sink_terms.append(jnp.exp(sink - m))
        p_s[rows, :BLOCK] = jnp.where(causal, 0.0, e).astype(BF16)
        p_s[rows, BLOCK:] = jnp.where(causal, e, 0.0).astype(BF16)
    return sink_terms


def _attn_values(j, g, sink_terms, p_s, kv_s, sg_s, y_ref):
    col = lax.broadcasted_iota(jnp.int32, (BLOCK, BLOCK), 1)
    lo_half = col < ATT_HEAD_DIM
    rows = slice(j * BLOCK, (j + 1) * BLOCK)
    vcat = kv_s[j * BLOCK:(j + 2) * BLOCK, (2 + g) * LANES:(3 + g) * LANES]
    o = _dot(p_s[...], vcat)
    for pr in range(ATT_GROUP // 2):
        he, ho = 2 * pr, 2 * pr + 1
        oe = o[he * BLOCK:(he + 1) * BLOCK]
        oo = o[ho * BLOCK:(ho + 1) * BLOCK]
        num = jnp.where(lo_half, oe, pltpu.roll(oo, ATT_HEAD_DIM, 1))
        den = jnp.where(lo_half, pltpu.roll(oe, ATT_HEAD_DIM, 1), oo) \
            + jnp.where(lo_half, sink_terms[he], sink_terms[ho])
        c0 = (g * (ATT_GROUP // 2) + pr) * LANES
        y = num * (1.0 / den) * sg_s[rows, c0:c0 + LANES].astype(F32)
        y_ref[rows, c0:c0 + LANES] = y.astype(BF16)


def _gla_independent(c, qd_s, ki_s, kt_s, gv_s):
    sl = slice(c * GLA_CHUNK, (c + 1) * GLA_CHUNK)
    out = []
    for h in range(GLA_HEADS):
        kc = slice(h * GLA_K_DIM, (h + 1) * GLA_K_DIM)
        vc = slice(h * GLA_V_DIM, (h + 1) * GLA_V_DIM)
        out.append((_dot_nt(qd_s[sl, kc], ki_s[sl, kc]), _dot_tn(kt_s[sl, kc], gv_s[sl, vc])))
    return out


def _gla_dependent(c, indep, qd_s, gv_s, sg_s, dk_s, gn_ref, st_s, y_ref):
    row = lax.broadcasted_iota(jnp.int32, (GLA_CHUNK, GLA_CHUNK), 0)
    col = lax.broadcasted_iota(jnp.int32, (GLA_CHUNK, GLA_CHUNK), 1)
    tril = col <= row
    gn = gn_ref[...]
    sl = slice(c * GLA_CHUNK, (c + 1) * GLA_CHUNK)
    for h in range(GLA_HEADS):
        kc = slice(h * GLA_K_DIM, (h + 1) * GLA_K_DIM)
        vc = slice(h * GLA_V_DIM, (h + 1) * GLA_V_DIM)
        oc = slice(ATT_WIDTH + h * GLA_V_DIM, ATT_WIDTH + (h + 1) * GLA_V_DIM)
        scores, kv_inc = indep[h]
        st = st_s[h]
        a = jnp.where(tril, scores, 0.0).astype(BF16)
        o = _dot(jnp.concatenate([qd_s[sl, kc], a], axis=1),
                 jnp.concatenate([st.astype(BF16), gv_s[sl, vc]], axis=0))
        st_s[h] = st * dk_s[kc, c:c + 1] + kv_inc
        y = _rms(o, gn) * sg_s[sl, oc].astype(F32)
        y_ref[sl, oc] = y.astype(BF16)


def _mixer_kernel(sinks_ref, x_ref, pos_ref, nmix_ref, w_ref, wup_ref, bg_ref, rope_ref, ltri_ref, gn_ref,
                  y_ref, u_s, trig_s, glow_s, lg_s, b_s, qstack_s, score_s, p_s,
                  q_s, kv_s, qd_s, ki_s, kt_s, gv_s, sg_s, dk_s, st_s, *,
                  tiles_per_seq):
    i = pl.program_id(0)
    tm = x_ref.shape[0]
    first = lax.rem(i - 1, tiles_per_seq) == 0

    @pl.when(i == 0)
    def _():
        for s in (q_s, kv_s, qd_s, ki_s, kt_s, gv_s, sg_s, dk_s):
            s[1] = jnp.zeros(s.shape[1:], s.dtype)

    @pl.when(first | (i == 0))
    def _():
        st_s[...] = jnp.zeros_like(st_s)

    def step(wr):
        rd = 1 - wr
        inproj = _inproj_pieces(
            x_ref, pos_ref, nmix_ref, w_ref, wup_ref, bg_ref, rope_ref, ltri_ref, u_s, trig_s, glow_s, lg_s, b_s,
            q_s.at[wr], kv_s.at[wr], qd_s.at[wr], ki_s.at[wr], kt_s.at[wr], gv_s.at[wr], sg_s.at[wr],
            dk_s.at[wr])
        q_r, kv_r, sg_r, gv_r, qd_r = q_s.at[rd], kv_s.at[rd], sg_s.at[rd], gv_s.at[rd], qd_s.at[rd]
        first_bias = jnp.where(first, -jnp.inf, 0.0).astype(F32)

        inproj(("norm",))
        for r, (long_group, short_group) in enumerate(_INPROJ_ROUNDS):
            j, g = divmod(r, ATT_KV_HEADS)
            _attn_scores(j, g, q_r, kv_r, qstack_s, score_s)
            sink_terms = _attn_softmax(g, first_bias if j == 0 else None, sinks_ref, score_s, p_s)
            gla_indep = _gla_independent(r, qd_r, ki_s.at[rd], kt_s.at[rd], gv_r)
            for spec in long_group:
                inproj(spec)
            _attn_values(j, g, sink_terms, p_s, kv_r, sg_r, y_ref)
            _gla_dependent(r, gla_indep, qd_r, gv_r, sg_r, dk_s.at[rd], gn_ref, st_s, y_ref)
            for spec in short_group:
                inproj(spec)
        kv_s[wr, 0:BLOCK, :] = kv_s[rd, tm:tm + BLOCK, :]

    assert len(_INPROJ_ROUNDS) == (tm // BLOCK) * ATT_KV_HEADS == tm // GLA_CHUNK
    parity = lax.rem(i, 2)
    for wr in range(2):
        pl.when(parity == wr)(functools.partial(step, wr))


def _wprep_source_row(c):
    n_head = C_GL // W_BLK
    tail_src = C_GL + GLA_GATE_RANK
    return jnp.where(c <= n_head, c * W_BLK, tail_src + (c - n_head - 1) * W_BLK)


def _wprep_kernel(*refs):
    *wt_refs, o_ref = refs
    for k, wt_ref in enumerate(wt_refs):
        c = pl.program_id(0) * WPREP_BLKS + k
        feat = _wprep_source_row(c) + lax.broadcasted_iota(jnp.int32, (W_BLK, 1), 0)
        scale = jnp.where(feat < ATT_WIDTH, ATT_HEAD_DIM ** -0.5, 1.0)
        padding = (c == C_GL // W_BLK) & (feat >= C_GL + GLA_GATE_RANK)
        o_ref[:, k * W_BLK:(k + 1) * W_BLK] = jnp.where(padding, 0.0, wt_ref[...] * scale).T.astype(BF16)


def _out_kernel(x_ref, y_ref, p_ref, wo_ref, pn_ref, wpg_ref, wpp_ref, fn_ref, o_ref, wo_s, wpg_s, wpp_s):
    @pl.when(pl.program_id(0) == 0)
    def _():
        wo_s[...] = wo_ref[...].astype(BF16)
        wpg_s[...] = wpg_ref[...].astype(BF16)
        wpp_s[...] = wpp_ref[...].astype(BF16)

    tm = x_ref.shape[0]
    halves = [slice(k * (tm // 2), (k + 1) * (tm // 2)) for k in range(2)]
    hs = [x_ref[r, :] + _dot(y_ref[r, :], wo_s[...]) for r in halves]
    ples = [_dot(p_ref[r, :].astype(BF16), wpp_s[...]) for r in halves]
    gates = [_dot(_rms(h, pn_ref[...]).astype(BF16), wpg_s[...]) for h in hs]
    for r, h, gate, ple in zip(halves, hs, gates, ples):
        o_ref[r, :] = _rms(h + _sigmoid(gate) * ple, fn_ref[...])


def _const_spec(shape):
    nd = len(shape)
    return pl.BlockSpec(shape, lambda *_: (0,) * nd, pipeline_mode=pl.Buffered(1))


def _rope_freqs():
    assert ROPE_HALF == SUBLANES
    inv_freq = ROPE_THETA ** (-jnp.arange(0, ROPE_DIM, 2, dtype=F32) / ROPE_DIM)
    return jnp.broadcast_to(inv_freq[:, None], (ROPE_HALF, LANES))


def _chunk_tril(n):
    i = np.arange(n)
    same = (i[:, None] // GLA_CHUNK) == (i[None, :] // GLA_CHUNK)
    return (same & (i[None, :] <= i[:, None])).astype(np.float32)


def _layer(h, p, pos, norm_mix, w_in, sinks, w_gate_up, b_gate, gla_norm, w_out, ple_norm,
           w_ple_gate, w_ple_proj, out_norm, batch, seq):
    n = batch * seq
    assert w_in.shape == (D_MODEL, D_IN_PROJ)
    w_t = jnp.swapaxes(w_in, 0, 1)
    src_spec = lambda k: pl.BlockSpec(
        (pl.Element(W_BLK), pl.Element(D_MODEL)),
        lambda c: (pl.multiple_of(_wprep_source_row(c * WPREP_BLKS + k), SUBLANES), 0))
    w_r = pl.pallas_call(
        _wprep_kernel,
        grid=(C_END // (WPREP_BLKS * W_BLK),),
        in_specs=[src_spec(k) for k in range(WPREP_BLKS)],
        out_specs=pl.BlockSpec((D_MODEL, WPREP_BLKS * W_BLK), lambda c: (0, c)),
        out_shape=jax.ShapeDtypeStruct((D_MODEL, C_END), BF16),
        compiler_params=pltpu.CompilerParams(dimension_semantics=("parallel",)),
        name="wprep",
    )(*([w_t] * WPREP_BLKS))
    wup = jnp.concatenate([w_gate_up, jnp.zeros((LANES - GLA_GATE_RANK, GLA_QK_WIDTH), w_gate_up.dtype)],
                          axis=0).astype(BF16)
    rope_freqs = _rope_freqs()
    ltri = jnp.asarray(_chunk_tril(CUM_BLK), dtype=BF16)

    params = lambda sem: pltpu.CompilerParams(dimension_semantics=sem, vmem_limit_bytes=VMEM_LIMIT)

    nt = n // TM_MIX
    in_tile = lambda i: (jnp.minimum(i, nt - 1), 0)
    out_tile = lambda i: (jnp.maximum(i - 1, 0), 0)
    two = lambda *shape: (2,) + shape
    y = pl.pallas_call(
        functools.partial(_mixer_kernel, tiles_per_seq=seq // TM_MIX),
        grid=(nt + 1,),
        in_specs=[
            pl.BlockSpec(memory_space=pltpu.SMEM),
            pl.BlockSpec((TM_MIX, D_MODEL), in_tile),
            pl.BlockSpec((1, TM_MIX // LANES, LANES), lambda i: (jnp.minimum(i, nt - 1), 0, 0)),
            _const_spec((1, D_MODEL)),
            _const_spec((D_MODEL, C_END)),
            _const_spec((LANES, GLA_QK_WIDTH)),
            _const_spec((1, GLA_QK_WIDTH)),
            _const_spec((SUBLANES, LANES)),
            _const_spec((CUM_BLK, CUM_BLK)),
            _const_spec((1, GLA_V_DIM)),
        ],
        out_specs=pl.BlockSpec((TM_MIX, ATT_WIDTH + GLA_WIDTH), out_tile),
        out_shape=jax.ShapeDtypeStruct((n, ATT_WIDTH + GLA_WIDTH), BF16),
        scratch_shapes=[
            pltpu.VMEM((TM_MIX, D_MODEL), BF16),
            pltpu.VMEM((3, TM_MIX, LANES), F32),
            pltpu.VMEM((TM_MIX, LANES), BF16),
            pltpu.VMEM((2, TM_MIX, GLA_QK_WIDTH), BF16),
            pltpu.VMEM((2, TM_MIX, GLA_QK_WIDTH), F32),
            pltpu.VMEM((ATT_GROUP * BLOCK, LANES), BF16),
            pltpu.VMEM((ATT_GROUP * BLOCK, 2 * BLOCK), F32),
            pltpu.VMEM((ATT_GROUP * BLOCK, 2 * BLOCK), BF16),
            pltpu.VMEM(two(TM_MIX, ATT_WIDTH), BF16),
            pltpu.VMEM(two(TM_MIX + BLOCK, 4 * ATT_KV_WIDTH), BF16),
            pltpu.VMEM(two(TM_MIX, GLA_QK_WIDTH), BF16),
            pltpu.VMEM(two(TM_MIX, GLA_QK_WIDTH), BF16),
            pltpu.VMEM(two(TM_MIX, GLA_QK_WIDTH), BF16),
            pltpu.VMEM(two(TM_MIX, GLA_WIDTH), BF16),
            pltpu.VMEM(two(TM_MIX, ATT_WIDTH + GLA_WIDTH), BF16),
            pltpu.VMEM(two(GLA_QK_WIDTH, LANES), F32),
            pltpu.VMEM((GLA_HEADS, GLA_K_DIM, GLA_V_DIM), F32),
        ],
        compiler_params=params(("arbitrary",)),
        name="mixer",
    )(sinks.astype(F32), h, pos.reshape(nt, TM_MIX // LANES, LANES), norm_mix.reshape(1, D_MODEL), w_r, wup,
      b_gate.reshape(1, GLA_QK_WIDTH), rope_freqs, ltri, gla_norm.reshape(1, GLA_V_DIM))

    row = lambda i: (i, 0)
    return pl.pallas_call(
        _out_kernel,
        grid=(n // TM_OUT,),
        in_specs=[
            pl.BlockSpec((TM_OUT, D_MODEL), row),
            pl.BlockSpec((TM_OUT, ATT_WIDTH + GLA_WIDTH), row),
            pl.BlockSpec((TM_OUT, D_PLE), row),
            _const_spec((ATT_WIDTH + GLA_WIDTH, D_MODEL)),
            _const_spec((1, D_MODEL)),
            _const_spec((D_MODEL, D_MODEL)),
            _const_spec((D_PLE, D_MODEL)),
            _const_spec((1, D_MODEL)),
        ],
        out_specs=pl.BlockSpec((TM_OUT, D_MODEL), row),
        out_shape=jax.ShapeDtypeStruct((n, D_MODEL), F32),
        scratch_shapes=[
            pltpu.VMEM((ATT_WIDTH + GLA_WIDTH, D_MODEL), BF16),
            pltpu.VMEM((D_MODEL, D_MODEL), BF16),
            pltpu.VMEM((D_PLE, D_MODEL), BF16),
        ],
        compiler_params=params(("arbitrary",)),
        name="outproj",
    )(h, y, p, w_out, ple_norm.reshape(1, D_MODEL), w_ple_gate, w_ple_proj, out_norm.reshape(1, D_MODEL))


def kernel(x, p, positions, norm_mix, w_in, attn_sinks, w_gate_up, b_gate, gla_norm, w_out, ple_norm,
           w_ple_gate, w_ple_proj, final_norm):
    batch, seq, _ = x.shape
    depth = w_in.shape[0]
    assert depth == 1, "the final RMSNorm is fused into the single layer's output kernel"
    n = batch * seq
    layer = lambda a: a.reshape(a.shape[1:])
    out = _layer(x.reshape(n, D_MODEL), p.reshape(n, D_PLE), positions.reshape(n), layer(norm_mix), layer(w_in),
                 layer(attn_sinks), layer(w_gate_up), layer(b_gate), layer(gla_norm), layer(w_out),
                 layer(ple_norm), layer(w_ple_gate), layer(w_ple_proj), final_norm, batch, seq)
    return out.reshape(batch, seq, D_MODEL)
```

```python
import functools

import jax
import jax.numpy as jnp
import numpy as np
from jax import lax
from jax.experimental import pallas as pl
from jax.experimental.pallas import tpu as pltpu

D_MODEL = 1024
D_PLE = 256
ATT_HEADS = 16
ATT_KV_HEADS = 2
ATT_HEAD_DIM = 64
ATT_WIDTH = ATT_HEADS * ATT_HEAD_DIM
ATT_KV_WIDTH = ATT_KV_HEADS * ATT_HEAD_DIM
ATT_GROUP = ATT_HEADS // ATT_KV_HEADS
BLOCK = 128
ROPE_DIM = ATT_HEAD_DIM // 4
ROPE_HALF = ROPE_DIM // 2
ROPE_THETA = 500000.0
GLA_HEADS = 4
GLA_V_DIM = 256
GLA_K_DIM = 128
GLA_WIDTH = GLA_HEADS * GLA_V_DIM
GLA_QK_WIDTH = GLA_HEADS * GLA_K_DIM
GLA_GATE_RANK = 16
GLA_GATE_NORMALIZER = 16.0
GLA_CHUNK = 64
EPS = 1e-6

LANES = 128
SUBLANES = 8
VMEM_LIMIT = 56 * 1024 * 1024

W_BLK = 256
WPREP_BLKS = 2

C_Q = 0
C_KV = C_Q + ATT_WIDTH
C_GQ = C_KV + 2 * ATT_KV_WIDTH
C_GK = C_GQ + GLA_QK_WIDTH
C_GV = C_GK + GLA_QK_WIDTH
C_GL = C_GV + GLA_WIDTH
C_ZA = C_GL + W_BLK
C_ZG = C_ZA + ATT_WIDTH
C_END = C_ZG + GLA_WIDTH
D_IN_PROJ = C_END - W_BLK + GLA_GATE_RANK

TM_MIX = 256
TM_OUT = 512
CUM_BLK = 256

BF16 = jnp.bfloat16
F32 = jnp.float32


def _dot(a, b):
    return jnp.dot(a, b, preferred_element_type=F32)


def _dot_nt(a, b):
    return lax.dot_general(a, b, (((1,), (1,)), ((), ())), preferred_element_type=F32)


def _dot_tn(a, b):
    return lax.dot_general(a, b, (((0,), (0,)), ((), ())), preferred_element_type=F32)


def _rms(x, g):
    return x * lax.rsqrt(jnp.mean(x * x, axis=-1, keepdims=True) + EPS) * g


def _sigmoid(z):
    return 0.5 * (1.0 + jnp.tanh(0.5 * z))


def _inproj_pieces(x_ref, pos_ref, nmix_ref, w_ref, wup_ref, bg_ref, rope_ref, ltri_ref,
                   u_s, trig_s, glow_s, lg_s, b_s, q_s, kv_s, qd_s, ki_s, kt_s, gv_s, sg_s, dk_s):
    tm = x_ref.shape[0]

    def proj(c0, width):
        return _dot(u_s[...], w_ref[:, c0:c0 + width])

    def norm():
        u_s[...] = _rms(x_ref[...], nmix_ref[...]).astype(BF16)
        pos = pos_ref[0].astype(F32)
        ones = jnp.ones((ATT_HEAD_DIM - ROPE_DIM, LANES), F32)
        zeros = jnp.zeros((ATT_HEAD_DIM - ROPE_DIM, LANES), F32)
        zero8 = jnp.zeros((ROPE_HALF, LANES), F32)
        for r in range(tm // LANES):
            ang = rope_ref[...] * pos[r:r + 1, :]
            c8 = jnp.cos(ang)
            s8 = jnp.sin(ang)
            rows = slice(r * LANES, (r + 1) * LANES)
            trig_s[0, rows, :] = jnp.concatenate([c8, c8, ones] * 2, axis=0).T
            trig_s[1, rows, :] = jnp.concatenate([-s8, zero8, zeros] * 2, axis=0).T
            trig_s[2, rows, :] = jnp.concatenate([zero8, s8, zeros] * 2, axis=0).T

    def rope(z):
        up = pltpu.roll(z, LANES - ROPE_HALF, 1)
        dn = pltpu.roll(z, ROPE_HALF, 1)
        return z * trig_s[0] + up * trig_s[1] + dn * trig_s[2]

    def q_piece(lo, width):
        z = proj(C_Q + lo, width)
        for l in range(0, width, LANES):
            q_s[:, lo + l:lo + l + LANES] = rope(z[:, l:l + LANES]).astype(BF16)

    def kv_piece(lo, width):
        assert (lo, width) == (0, 2 * ATT_KV_WIDTH)
        z = proj(C_KV, width)
        lo_half = lax.broadcasted_iota(jnp.int32, (tm, LANES), 1) < ATT_HEAD_DIM
        k = rope(z[:, :LANES])
        k_sw = pltpu.roll(k, ATT_HEAD_DIM, 1)
        v = z[:, LANES:]
        v_sw = pltpu.roll(v, ATT_HEAD_DIM, 1)
        kv_s[BLOCK:, 0 * LANES:1 * LANES] = jnp.where(lo_half, k, k_sw).astype(BF16)
        kv_s[BLOCK:, 1 * LANES:2 * LANES] = jnp.where(lo_half, k_sw, k).astype(BF16)
        kv_s[BLOCK:, 2 * LANES:3 * LANES] = jnp.where(lo_half, v, 1.0).astype(BF16)
        kv_s[BLOCK:, 3 * LANES:4 * LANES] = jnp.where(lo_half, v_sw, 1.0).astype(BF16)

    def gv_piece(lo, width):
        gv_s[:, lo:lo + width] = proj(C_GV + lo, width).astype(BF16)

    def sg_piece(lo, width):
        z = proj(C_ZA + lo, width)
        sg_s[:, lo:lo + width] = (z * _sigmoid(z)).astype(BF16)

    def gate_low():
        glow_s[...] = proj(C_GL, LANES).astype(BF16)

    def gate_logsig():
        logit = _dot(glow_s[...], wup_ref[...]) + bg_ref[...]
        lg = (jnp.minimum(logit, 0.0) - jnp.log(1.0 + jnp.exp(-jnp.abs(logit)))) * (1.0 / GLA_GATE_NORMALIZER)
        lg_hi = lg.astype(BF16)
        lg_s[0] = lg_hi
        lg_s[1] = (lg - lg_hi.astype(F32)).astype(BF16)

    def gate_cumsum():
        ltri = ltri_ref[...]
        nb = CUM_BLK // GLA_CHUNK
        chunk_totals = []
        for r in range(0, tm, CUM_BLK):
            b = _dot(ltri, lg_s[0, r:r + CUM_BLK, :]) + _dot(ltri, lg_s[1, r:r + CUM_BLK, :])
            b3 = b.reshape(nb, GLA_CHUNK, GLA_QK_WIDTH)
            b_last = b3[:, GLA_CHUNK - 1:GLA_CHUNK, :]
            b_s[0, r:r + CUM_BLK, :] = b
            b_s[1, r:r + CUM_BLK, :] = (b_last - b3).reshape(CUM_BLK, GLA_QK_WIDTH)
            chunk_totals.append(b_last.reshape(nb, GLA_QK_WIDTH))
        pad = jnp.zeros((LANES - tm // GLA_CHUNK, GLA_QK_WIDTH), F32)
        decay = jnp.exp(jnp.concatenate(chunk_totals + [pad], axis=0))
        for h in range(GLA_HEADS):
            kc = slice(h * GLA_K_DIM, (h + 1) * GLA_K_DIM)
            dk_s[kc, :] = decay[:, kc].T

    def gq_piece(lo, width):
        z = proj(C_GQ + lo, width) * (GLA_K_DIM ** -0.5)
        qd_s[:, lo:lo + width] = (z * jnp.exp(b_s[0, :, lo:lo + width])).astype(BF16)

    def gk_piece(lo, width):
        z = proj(C_GK + lo, width)
        ki_s[:, lo:lo + width] = (z * jnp.exp(-b_s[0, :, lo:lo + width])).astype(BF16)
        kt_s[:, lo:lo + width] = (z * jnp.exp(b_s[1, :, lo:lo + width])).astype(BF16)

    column_pieces = {"q": q_piece, "kv": kv_piece, "gv": gv_piece, "sg": sg_piece, "gq": gq_piece, "gk": gk_piece}
    single_pieces = {"norm": norm, "gate_low": gate_low, "gate_logsig": gate_logsig, "gate_cumsum": gate_cumsum}

    def run(spec):
        if spec[0] in single_pieces:
            single_pieces[spec[0]]()
        else:
            column_pieces[spec[0]](spec[1], spec[2])
    return run


_INPROJ_ROUNDS = (
    ((("q", 0, 512), ("q", 512, 512)), (("kv", 0, 256), ("gate_low",))),
    ((("gv", 0, 512), ("gv", 512, 512)), (("gate_logsig",),)),
    ((("sg", 0, 512), ("sg", 512, 512)), (("gate_cumsum",),)),
    ((("sg", 1024, 512), ("gq", 0, 512)), (("sg", 1536, 512), ("gk", 0, 512))),
)


def _attn_scores(j, g, q_s, kv_s, qstack_s, score_s):
    col = lax.broadcasted_iota(jnp.int32, (BLOCK, BLOCK), 1)
    lo_half = col < ATT_HEAD_DIM
    zero = jnp.zeros((), BF16)
    kcat = kv_s[j * BLOCK:(j + 2) * BLOCK, g * LANES:(g + 1) * LANES]
    for pr in range(ATT_GROUP // 2):
        c0 = (g * (ATT_GROUP // 2) + pr) * LANES
        qp = q_s[j * BLOCK:(j + 1) * BLOCK, c0:c0 + LANES]
        qstack_s[(2 * pr) * BLOCK:(2 * pr + 1) * BLOCK, :] = jnp.where(lo_half, qp, zero)
        qstack_s[(2 * pr + 1) * BLOCK:(2 * pr + 2) * BLOCK, :] = jnp.where(lo_half, zero, qp)
    score_s[...] = _dot_nt(qstack_s[...], kcat)


def _attn_softmax(g, prev_bias, sinks_ref, score_s, p_s):
    row = lax.broadcasted_iota(jnp.int32, (BLOCK, BLOCK), 0)
    col = lax.broadcasted_iota(jnp.int32, (BLOCK, BLOCK), 1)
    causal = col <= row
    sink_terms = []
    for h in range(ATT_GROUP):
        rows = slice(h * BLOCK, (h + 1) * BLOCK)
        s_prev = score_s[rows, :BLOCK]
        if prev_bias is not None:
            s_prev = s_prev + prev_bias
        sf = jnp.where(causal, score_s[rows, BLOCK:], s_prev)
        sink = sinks_ref[g * ATT_GROUP + h]
        m = jnp.maximum(jnp.max(sf, axis=-1, keepdims=True), sink)
        e = jnp.exp(sf - m)
        sink_terms.append(jnp.exp(sink - m))
        p_s[rows, :BLOCK] = jnp.where(causal, 0.0, e).astype(BF16)
        p_s[rows, BLOCK:] = jnp.where(causal, e, 0.0).astype(BF16)
    return sink_terms


def _attn_values(j, g, sink_terms, p_s, kv_s, sg_s, y_ref):
    col = lax.broadcasted_iota(jnp.int32, (BLOCK, BLOCK), 1)
    lo_half = col < ATT_HEAD_DIM
    rows = slice(j * BLOCK, (j + 1) * BLOCK)
    vcat = kv_s[j * BLOCK:(j + 2) * BLOCK, (2 + g) * LANES:(3 + g) * LANES]
    o = _dot(p_s[...], vcat)
    for pr in range(ATT_GROUP // 2):
        he, ho = 2 * pr, 2 * pr + 1
        oe = o[he * BLOCK:(he + 1) * BLOCK]
        oo = o[ho * BLOCK:(ho + 1) * BLOCK]
        num = jnp.where(lo_half, oe, pltpu.roll(oo, ATT_HEAD_DIM, 1))
        den = jnp.where(lo_half, pltpu.roll(oe, ATT_HEAD_DIM, 1), oo) \
            + jnp.where(lo_half, sink_terms[he], sink_terms[ho])
        c0 = (g * (ATT_GROUP // 2) + pr) * LANES
        y = num * (1.0 / den) * sg_s[rows, c0:c0 + LANES].astype(F32)
        y_ref[rows, c0:c0 + LANES] = y.astype(BF16)


def _gla_independent(c, qd_s, ki_s, kt_s, gv_s):
    sl = slice(c * GLA_CHUNK, (c + 1) * GLA_CHUNK)
    out = []
    for h in range(GLA_HEADS):
        kc = slice(h * GLA_K_DIM, (h + 1) * GLA_K_DIM)
        vc = slice(h * GLA_V_DIM, (h + 1) * GLA_V_DIM)
        out.append((_dot_nt(qd_s[sl, kc], ki_s[sl, kc]), _dot_tn(kt_s[sl, kc], gv_s[sl, vc])))
    return out


def _gla_dependent(c, indep, qd_s, gv_s, sg_s, dk_s, gn_ref, st_s, y_ref):
    row = lax.broadcasted_iota(jnp.int32, (GLA_CHUNK, GLA_CHUNK), 0)
    col = lax.broadcasted_iota(jnp.int32, (GLA_CHUNK, GLA_CHUNK), 1)
    tril = col <= row
    gn = gn_ref[...]
    sl = slice(c * GLA_CHUNK, (c + 1) * GLA_CHUNK)
    for h in range(GLA_HEADS):
        kc = slice(h * GLA_K_DIM, (h + 1) * GLA_K_DIM)
        vc = slice(h * GLA_V_DIM, (h + 1) * GLA_V_DIM)
        oc = slice(ATT_WIDTH + h * GLA_V_DIM, ATT_WIDTH + (h + 1) * GLA_V_DIM)
        scores, kv_inc = indep[h]
        st = st_s[h]
        a = jnp.where(tril, scores, 0.0).astype(BF16)
        o = _dot(jnp.concatenate([qd_s[sl, kc], a], axis=1),
                 jnp.concatenate([st.astype(BF16), gv_s[sl, vc]], axis=0))
        st_s[h] = st * dk_s[kc, c:c + 1] + kv_inc
        y = _rms(o, gn) * sg_s[sl, oc].astype(F32)
        y_ref[sl, oc] = y.astype(BF16)


def _mixer_kernel(sinks_ref, x_ref, pos_ref, nmix_ref, w_ref, wup_ref, bg_ref, rope_ref, ltri_ref, gn_ref,
                  y_ref, u_s, trig_s, glow_s, lg_s, b_s, qstack_s, score_s, p_s,
                  q_s, kv_s, qd_s, ki_s, kt_s, gv_s, sg_s, dk_s, st_s, *,
                  tiles_per_seq):
    i = pl.program_id(0)
    tm = x_ref.shape[0]
    first = lax.rem(i - 1, tiles_per_seq) == 0

    @pl.when(i == 0)
    def _():
        for s in (q_s, kv_s, qd_s, ki_s, kt_s, gv_s, sg_s, dk_s):
            s[1] = jnp.zeros(s.shape[1:], s.dtype)

    @pl.when(first | (i == 0))
    def _():
        st_s[...] = jnp.zeros_like(st_s)

    def step(wr):
        rd = 1 - wr
        inproj = _inproj_pieces(
            x_ref, pos_ref, nmix_ref, w_ref, wup_ref, bg_ref, rope_ref, ltri_ref, u_s, trig_s, glow_s, lg_s, b_s,
            q_s.at[wr], kv_s.at[wr], qd_s.at[wr], ki_s.at[wr], kt_s.at[wr], gv_s.at[wr], sg_s.at[wr],
            dk_s.at[wr])
        q_r, kv_r, sg_r, gv_r, qd_r = q_s.at[rd], kv_s.at[rd], sg_s.at[rd], gv_s.at[rd], qd_s.at[rd]
        first_bias = jnp.where(first, -jnp.inf, 0.0).astype(F32)

        inproj(("norm",))
        for r, (long_group, short_group) in enumerate(_INPROJ_ROUNDS):
            j, g = divmod(r, ATT_KV_HEADS)
            _attn_scores(j, g, q_r, kv_r, qstack_s, score_s)
            sink_terms = _attn_softmax(g, first_bias if j == 0 else None, sinks_ref, score_s, p_s)
            gla_indep = _gla_independent(r, qd_r, ki_s.at[rd], kt_s.at[rd], gv_r)
            for spec in long_group:
                inproj(spec)
            _attn_values(j, g, sink_terms, p_s, kv_r, sg_r, y_ref)
            _gla_dependent(r, gla_indep, qd_r, gv_r, sg_r, dk_s.at[rd], gn_ref, st_s, y_ref)
            for spec in short_group:
                inproj(spec)
        kv_s[wr, 0:BLOCK, :] = kv_s[rd, tm:tm + BLOCK, :]

    assert len(_INPROJ_ROUNDS) == (tm // BLOCK) * ATT_KV_HEADS == tm // GLA_CHUNK
    parity = lax.rem(i, 2)
    for wr in range(2):
        pl.when(parity == wr)(functools.partial(step, wr))


def _wprep_source_row(c):
    n_head = C_GL // W_BLK
    tail_src = C_GL + GLA_GATE_RANK
    return jnp.where(c <= n_head, c * W_BLK, tail_src + (c - n_head - 1) * W_BLK)


def _wprep_kernel(*refs):
    *wt_refs, o_ref = refs
    for k, wt_ref in enumerate(wt_refs):
        c = pl.program_id(0) * WPREP_BLKS + k
        feat = _wprep_source_row(c) + lax.broadcasted_iota(jnp.int32, (W_BLK, 1), 0)
        scale = jnp.where(feat < ATT_WIDTH, ATT_HEAD_DIM ** -0.5, 1.0)
        padding = (c == C_GL // W_BLK) & (feat >= C_GL + GLA_GATE_RANK)
        o_ref[:, k * W_BLK:(k + 1) * W_BLK] = jnp.where(padding, 0.0, wt_ref[...] * scale).T.astype(BF16)


def _out_kernel(x_ref, y_ref, p_ref, wo_ref, pn_ref, wpg_ref, wpp_ref, fn_ref, o_ref, wo_s, wpg_s, wpp_s):
    @pl.when(pl.program_id(0) == 0)
    def _():
        wo_s[...] = wo_ref[...].astype(BF16)
        wpg_s[...] = wpg_ref[...].astype(BF16)
        wpp_s[...] = wpp_ref[...].astype(BF16)

    tm = x_ref.shape[0]
    halves = [slice(k * (tm // 2), (k + 1) * (tm // 2)) for k in range(2)]
    hs = [x_ref[r, :] + _dot(y_ref[r, :], wo_s[...]) for r in halves]
    ples = [_dot(p_ref[r, :].astype(BF16), wpp_s[...]) for r in halves]
    gates = [_dot(_rms(h, pn_ref[...]).astype(BF16), wpg_s[...]) for h in hs]
    for r, h, gate, ple in zip(halves, hs, gates, ples):
        o_ref[r, :] = _rms(h + _sigmoid(gate) * ple, fn_ref[...])


def _const_spec(shape):
    nd = len(shape)
    return pl.BlockSpec(shape, lambda *_: (0,) * nd, pipeline_mode=pl.Buffered(1))


def _rope_freqs():
    assert ROPE_HALF == SUBLANES
    inv_freq = ROPE_THETA ** (-jnp.arange(0, ROPE_DIM, 2, dtype=F32) / ROPE_DIM)
    return jnp.broadcast_to(inv_freq[:, None], (ROPE_HALF, LANES))


def _chunk_tril(n):
    i = np.arange(n)
    same = (i[:, None] // GLA_CHUNK) == (i[None, :] // GLA_CHUNK)
    return (same & (i[None, :] <= i[:, None])).astype(np.float32)


def _layer(h, p, pos, norm_mix, w_in, sinks, w_gate_up, b_gate, gla_norm, w_out, ple_norm,
           w_ple_gate, w_ple_proj, out_norm, batch, seq):
    n = batch * seq
    assert w_in.shape == (D_MODEL, D_IN_PROJ)
    w_t = jnp.swapaxes(w_in, 0, 1)
    src_spec = lambda k: pl.BlockSpec(
        (pl.Element(W_BLK), pl.Element(D_MODEL)),
        lambda c: (pl.multiple_of(_wprep_source_row(c * WPREP_BLKS + k), SUBLANES), 0))
    w_r = pl.pallas_call(
        _wprep_kernel,
        grid=(C_END // (WPREP_BLKS * W_BLK),),
        in_specs=[src_spec(k) for k in range(WPREP_BLKS)],
        out_specs=pl.BlockSpec((D_MODEL, WPREP_BLKS * W_BLK), lambda c: (0, c)),
        out_shape=jax.ShapeDtypeStruct((D_MODEL, C_END), BF16),
        compiler_params=pltpu.CompilerParams(dimension_semantics=("parallel",)),
        name="wprep",
    )(*([w_t] * WPREP_BLKS))
    wup = jnp.concatenate([w_gate_up, jnp.zeros((LANES - GLA_GATE_RANK, GLA_QK_WIDTH), w_gate_up.dtype)],
                          axis=0).astype(BF16)
    rope_freqs = _rope_freqs()
    ltri = jnp.asarray(_chunk_tril(CUM_BLK), dtype=BF16)

    params = lambda sem: pltpu.CompilerParams(dimension_semantics=sem, vmem_limit_bytes=VMEM_LIMIT)

    nt = n // TM_MIX
    in_tile = lambda i: (jnp.minimum(i, nt - 1), 0)
    out_tile = lambda i: (jnp.maximum(i - 1, 0), 0)
    two = lambda *shape: (2,) + shape
    y = pl.pallas_call(
        functools.partial(_mixer_kernel, tiles_per_seq=seq // TM_MIX),
        grid=(nt + 1,),
        in_specs=[
            pl.BlockSpec(memory_space=pltpu.SMEM),
            pl.BlockSpec((TM_MIX, D_MODEL), in_tile),
            pl.BlockSpec((1, TM_MIX // LANES, LANES), lambda i: (jnp.minimum(i, nt - 1), 0, 0)),
            _const_spec((1, D_MODEL)),
            _const_spec((D_MODEL, C_END)),
            _const_spec((LANES, GLA_QK_WIDTH)),
            _const_spec((1, GLA_QK_WIDTH)),
            _const_spec((SUBLANES, LANES)),
            _const_spec((CUM_BLK, CUM_BLK)),
            _const_spec((1, GLA_V_DIM)),
        ],
        out_specs=pl.BlockSpec((TM_MIX, ATT_WIDTH + GLA_WIDTH), out_tile),
        out_shape=jax.ShapeDtypeStruct((n, ATT_WIDTH + GLA_WIDTH), BF16),
        scratch_shapes=[
            pltpu.VMEM((TM_MIX, D_MODEL), BF16),
            pltpu.VMEM((3, TM_MIX, LANES), F32),
            pltpu.VMEM((TM_MIX, LANES), BF16),
            pltpu.VMEM((2, TM_MIX, GLA_QK_WIDTH), BF16),
            pltpu.VMEM((2, TM_MIX, GLA_QK_WIDTH), F32),
            pltpu.VMEM((ATT_GROUP * BLOCK, LANES), BF16),
            pltpu.VMEM((ATT_GROUP * BLOCK, 2 * BLOCK), F32),
            pltpu.VMEM((ATT_GROUP * BLOCK, 2 * BLOCK), BF16),
            pltpu.VMEM(two(TM_MIX, ATT_WIDTH), BF16),
            pltpu.VMEM(two(TM_MIX + BLOCK, 4 * ATT_KV_WIDTH), BF16),
            pltpu.VMEM(two(TM_MIX, GLA_QK_WIDTH), BF16),
            pltpu.VMEM(two(TM_MIX, GLA_QK_WIDTH), BF16),
            pltpu.VMEM(two(TM_MIX, GLA_QK_WIDTH), BF16),
            pltpu.VMEM(two(TM_MIX, GLA_WIDTH), BF16),
            pltpu.VMEM(two(TM_MIX, ATT_WIDTH + GLA_WIDTH), BF16),
            pltpu.VMEM(two(GLA_QK_WIDTH, LANES), F32),
            pltpu.VMEM((GLA_HEADS, GLA_K_DIM, GLA_V_DIM), F32),
        ],
        compiler_params=params(("arbitrary",)),
        name="mixer",
    )(sinks.astype(F32), h, pos.reshape(nt, TM_MIX // LANES, LANES), norm_mix.reshape(1, D_MODEL), w_r, wup,
      b_gate.reshape(1, GLA_QK_WIDTH), rope_freqs, ltri, gla_norm.reshape(1, GLA_V_DIM))

    row = lambda i: (i, 0)
    return pl.pallas_call(
        _out_kernel,
        grid=(n // TM_OUT,),
        in_specs=[
            pl.BlockSpec((TM_OUT, D_MODEL), row),
            pl.BlockSpec((TM_OUT, ATT_WIDTH + GLA_WIDTH), row),
            pl.BlockSpec((TM_OUT, D_PLE), row),
            _const_spec((ATT_WIDTH + GLA_WIDTH, D_MODEL)),
            _const_spec((1, D_MODEL)),
            _const_spec((D_MODEL, D_MODEL)),
            _const_spec((D_PLE, D_MODEL)),
            _const_spec((1, D_MODEL)),
        ],
        out_specs=pl.BlockSpec((TM_OUT, D_MODEL), row),
        out_shape=jax.ShapeDtypeStruct((n, D_MODEL), F32),
        scratch_shapes=[
            pltpu.VMEM((ATT_WIDTH + GLA_WIDTH, D_MODEL), BF16),
            pltpu.VMEM((D_MODEL, D_MODEL), BF16),
            pltpu.VMEM((D_PLE, D_MODEL), BF16),
        ],
        compiler_params=params(("arbitrary",)),
        name="outproj",
    )(h, y, p, w_out, ple_norm.reshape(1, D_MODEL), w_ple_gate, w_ple_proj, out_norm.reshape(1, D_MODEL))


def kernel(x, p, positions, norm_mix, w_in, attn_sinks, w_gate_up, b_gate, gla_norm, w_out, ple_norm,
           w_ple_gate, w_ple_proj, final_norm):
    batch, seq, _ = x.shape
    depth = w_in.shape[0]
    assert depth == 1, "the final RMSNorm is fused into the single layer's output kernel"
    n = batch * seq
    layer = lambda a: a.reshape(a.shape[1:])
    out = _layer(x.reshape(n, D_MODEL), p.reshape(n, D_PLE), positions.reshape(n), layer(norm_mix), layer(w_in),
                 layer(attn_sinks), layer(w_gate_up), layer(b_gate), layer(gla_norm), layer(w_out),
                 layer(ple_norm), layer(w_ple_gate), layer(w_ple_proj), final_norm, batch, seq)
    return out.reshape(batch, seq, D_MODEL)
```

```python
import functools

import jax
import jax.numpy as jnp
import numpy as np
from jax import lax
from jax.experimental import pallas as pl
from jax.experimental.pallas import tpu as pltpu

D_MODEL = 1024
D_PLE = 256
ATT_HEADS = 16
ATT_KV_HEADS = 2
ATT_HEAD_DIM = 64
ATT_WIDTH = ATT_HEADS * ATT_HEAD_DIM
ATT_KV_WIDTH = ATT_KV_HEADS * ATT_HEAD_DIM
ATT_GROUP = ATT_HEADS // ATT_KV_HEADS
BLOCK = 128
ROPE_DIM = ATT_HEAD_DIM // 4
ROPE_HALF = ROPE_DIM // 2
ROPE_THETA = 500000.0
GLA_HEADS = 4
GLA_V_DIM = 256
GLA_K_DIM = 128
GLA_WIDTH = GLA_HEADS * GLA_V_DIM
GLA_QK_WIDTH = GLA_HEADS * GLA_K_DIM
GLA_GATE_RANK = 16
GLA_GATE_NORMALIZER = 16.0
GLA_CHUNK = 64
EPS = 1e-6

LANES = 128
SUBLANES = 8
VMEM_LIMIT = 56 * 1024 * 1024

W_BLK = 256
WPREP_BLKS = 2

C_Q = 0
C_KV = C_Q + ATT_WIDTH
C_GQ = C_KV + 2 * ATT_KV_WIDTH
C_GK = C_GQ + GLA_QK_WIDTH
C_GV = C_GK + GLA_QK_WIDTH
C_GL = C_GV + GLA_WIDTH
C_ZA = C_GL + W_BLK
C_ZG = C_ZA + ATT_WIDTH
C_END = C_ZG + GLA_WIDTH
D_IN_PROJ = C_END - W_BLK + GLA_GATE_RANK

TM_MIX = 256
TM_OUT = 512
CUM_BLK = 256

BF16 = jnp.bfloat16
F32 = jnp.float32


def _dot(a, b):
    return jnp.dot(a, b, preferred_element_type=F32)


def _dot_nt(a, b):
    return lax.dot_general(a, b, (((1,), (1,)), ((), ())), preferred_element_type=F32)


def _dot_tn(a, b):
    return lax.dot_general(a, b, (((0,), (0,)), ((), ())), preferred_element_type=F32)


def _rms(x, g):
    return x * lax.rsqrt(jnp.mean(x * x, axis=-1, keepdims=True) + EPS) * g


def _sigmoid(z):
    return 0.5 * (1.0 + jnp.tanh(0.5 * z))


def _inproj_pieces(x_ref, pos_ref, nmix_ref, w_ref, wup_ref, bg_ref, rope_ref, ltri_ref,
                   u_s, trig_s, glow_s, lg_s, b_s, q_s, kv_s, qd_s, ki_s, kt_s, gv_s, sg_s, dk_s):
    tm = x_ref.shape[0]

    def proj(c0, width):
        return _dot(u_s[...], w_ref[:, c0:c0 + width])

    def norm():
        u_s[...] = _rms(x_ref[...], nmix_ref[...]).astype(BF16)
        pos = pos_ref[0].astype(F32)
        ones = jnp.ones((ATT_HEAD_DIM - ROPE_DIM, LANES), F32)
        zeros = jnp.zeros((ATT_HEAD_DIM - ROPE_DIM, LANES), F32)
        zero8 = jnp.zeros((ROPE_HALF, LANES), F32)
        for r in range(tm // LANES):
            ang = rope_ref[...] * pos[r:r + 1, :]
            c8 = jnp.cos(ang)
            s8 = jnp.sin(ang)
            rows = slice(r * LANES, (r + 1) * LANES)
            trig_s[0, rows, :] = jnp.concatenate([c8, c8, ones] * 2, axis=0).T
            trig_s[1, rows, :] = jnp.concatenate([-s8, zero8, zeros] * 2, axis=0).T
            trig_s[2, rows, :] = jnp.concatenate([zero8, s8, zeros] * 2, axis=0).T

    def rope(z):
        up = pltpu.roll(z, LANES - ROPE_HALF, 1)
        dn = pltpu.roll(z, ROPE_HALF, 1)
        return z * trig_s[0] + up * trig_s[1] + dn * trig_s[2]

    def q_piece(lo, width):
        z = proj(C_Q + lo, width)
        for l in range(0, width, LANES):
            q_s[:, lo + l:lo + l + LANES] = rope(z[:, l:l + LANES]).astype(BF16)

    def kv_piece(lo, width):
        assert (lo, width) == (0, 2 * ATT_KV_WIDTH)
        z = proj(C_KV, width)
        lo_half = lax.broadcasted_iota(jnp.int32, (tm, LANES), 1) < ATT_HEAD_DIM
        k = rope(z[:, :LANES])
        k_sw = pltpu.roll(k, ATT_HEAD_DIM, 1)
        v = z[:, LANES:]
        v_sw = pltpu.roll(v, ATT_HEAD_DIM, 1)
        kv_s[BLOCK:, 0 * LANES:1 * LANES] = jnp.where(lo_half, k, k_sw).astype(BF16)
        kv_s[BLOCK:, 1 * LANES:2 * LANES] = jnp.where(lo_half, k_sw, k).astype(BF16)
        kv_s[BLOCK:, 2 * LANES:3 * LANES] = jnp.where(lo_half, v, 1.0).astype(BF16)
        kv_s[BLOCK:, 3 * LANES:4 * LANES] = jnp.where(lo_half, v_sw, 1.0).astype(BF16)

    def gv_piece(lo, width):
        gv_s[:, lo:lo + width] = proj(C_GV + lo, width).astype(BF16)

    def sg_piece(lo, width):
        z = proj(C_ZA + lo, width)
        sg_s[:, lo:lo + width] = (z * _sigmoid(z)).astype(BF16)

    def gate_low():
        glow_s[...] = proj(C_GL, LANES).astype(BF16)

    def gate_logsig():
        logit = _dot(glow_s[...], wup_ref[...]) + bg_ref[...]
        lg = (jnp.minimum(logit, 0.0) - jnp.log(1.0 + jnp.exp(-jnp.abs(logit)))) * (1.0 / GLA_GATE_NORMALIZER)
        lg_hi = lg.astype(BF16)
        lg_s[0] = lg_hi
        lg_s[1] = (lg - lg_hi.astype(F32)).astype(BF16)

    def gate_cumsum():
        ltri = ltri_ref[...]
        nb = CUM_BLK // GLA_CHUNK
        chunk_totals = []
        for r in range(0, tm, CUM_BLK):
            b = _dot(ltri, lg_s[0, r:r + CUM_BLK, :]) + _dot(ltri, lg_s[1, r:r + CUM_BLK, :])
            b3 = b.reshape(nb, GLA_CHUNK, GLA_QK_WIDTH)
            b_last = b3[:, GLA_CHUNK - 1:GLA_CHUNK, :]
            b_s[0, r:r + CUM_BLK, :] = b
            b_s[1, r:r + CUM_BLK, :] = (b_last - b3).reshape(CUM_BLK, GLA_QK_WIDTH)
            chunk_totals.append(b_last.reshape(nb, GLA_QK_WIDTH))
        pad = jnp.zeros((LANES - tm // GLA_CHUNK, GLA_QK_WIDTH), F32)
        decay = jnp.exp(jnp.concatenate(chunk_totals + [pad], axis=0))
        for h in range(GLA_HEADS):
            kc = slice(h * GLA_K_DIM, (h + 1) * GLA_K_DIM)
            dk_s[kc, :] = decay[:, kc].T

    def gq_piece(lo, width):
        z = proj(C_GQ + lo, width) * (GLA_K_DIM ** -0.5)
        qd_s[:, lo:lo + width] = (z * jnp.exp(b_s[0, :, lo:lo + width])).astype(BF16)

    def gk_piece(lo, width):
        z = proj(C_GK + lo, width)
        ki_s[:, lo:lo + width] = (z * jnp.exp(-b_s[0, :, lo:lo + width])).astype(BF16)
        kt_s[:, lo:lo + width] = (z * jnp.exp(b_s[1, :, lo:lo + width])).astype(BF16)

    column_pieces = {"q": q_piece, "kv": kv_piece, "gv": gv_piece, "sg": sg_piece, "gq": gq_piece, "gk": gk_piece}
    single_pieces = {"norm": norm, "gate_low": gate_low, "gate_logsig": gate_logsig, "gate_cumsum": gate_cumsum}

    def run(spec):
        if spec[0] in single_pieces:
            single_pieces[spec[0]]()
        else:
            column_pieces[spec[0]](spec[1], spec[2])
    return run


_INPROJ_ROUNDS = (
    ((("q", 0, 512), ("q", 512, 512)), (("kv", 0, 256), ("gate_low",))),
    ((("sg", 0, 512), ("sg", 512, 512)), (("gate_logsig",),)),
    ((("sg", 1024, 512), ("sg", 1536, 512)), (("gate_cumsum",),)),
    ((("gq", 0, 512), ("gk", 0, 512)), (("gv", 0, 512), ("gv", 512, 512))),
)


def _attn_scores(j, g, q_s, kv_s, qstack_s, score_s):
    col = lax.broadcasted_iota(jnp.int32, (BLOCK, BLOCK), 1)
    lo_half = col < ATT_HEAD_DIM
    zero = jnp.zeros((), BF16)
    kcat = kv_s[j * BLOCK:(j + 2) * BLOCK, g * LANES:(g + 1) * LANES]
    for pr in range(ATT_GROUP // 2):
        c0 = (g * (ATT_GROUP // 2) + pr) * LANES
        qp = q_s[j * BLOCK:(j + 1) * BLOCK, c0:c0 + LANES]
        qstack_s[(2 * pr) * BLOCK:(2 * pr + 1) * BLOCK, :] = jnp.where(lo_half, qp, zero)
        qstack_s[(2 * pr + 1) * BLOCK:(2 * pr + 2) * BLOCK, :] = jnp.where(lo_half, zero, qp)
    score_s[...] = _dot_nt(qstack_s[...], kcat)


def _attn_softmax(g, prev_bias, sinks_ref, score_s, p_s):
    row = lax.broadcasted_iota(jnp.int32, (BLOCK, BLOCK), 0)
    col = lax.broadcasted_iota(jnp.int32, (BLOCK, BLOCK), 1)
    causal = col <= row
    sink_terms = []
    for h in range(ATT_GROUP):
        rows = slice(h * BLOCK, (h + 1) * BLOCK)
        s_prev = score_s[rows, :BLOCK]
        if prev_bias is not None:
            s_prev = s_prev + prev_bias
        sf = jnp.where(causal, score_s[rows, BLOCK:], s_prev)
        sink = sinks_ref[g * ATT_GROUP + h]
        m = jnp.maximum(jnp.max(sf, axis=-1, keepdims=True), sink)
        e = jnp.exp(sf - m)
        sink_terms.append(jnp.exp(sink - m))
        p_s[rows, :BLOCK] = jnp.where(causal, 0.0, e).astype(BF16)
        p_s[rows, BLOCK:] = jnp.where(causal, e, 0.0).astype(BF16)
    return sink_terms


def _attn_values(j, g, sink_terms, p_s, kv_s, sg_s, y_ref):
    col = lax.broadcasted_iota(jnp.int32, (BLOCK, BLOCK), 1)
    lo_half = col < ATT_HEAD_DIM
    rows = slice(j * BLOCK, (j + 1) * BLOCK)
    vcat = kv_s[j * BLOCK:(j + 2) * BLOCK, (2 + g) * LANES:(3 + g) * LANES]
    o = _dot(p_s[...], vcat)
    for pr in range(ATT_GROUP // 2):
        he, ho = 2 * pr, 2 * pr + 1
        oe = o[he * BLOCK:(he + 1) * BLOCK]
        oo = o[ho * BLOCK:(ho + 1) * BLOCK]
        num = jnp.where(lo_half, oe, pltpu.roll(oo, ATT_HEAD_DIM, 1))
        den = jnp.where(lo_half, pltpu.roll(oe, ATT_HEAD_DIM, 1), oo) \
            + jnp.where(lo_half, sink_terms[he], sink_terms[ho])
        c0 = (g * (ATT_GROUP // 2) + pr) * LANES
        y = num * (1.0 / den) * sg_s[rows, c0:c0 + LANES].astype(F32)
        y_ref[rows, c0:c0 + LANES] = y.astype(BF16)


def _gla_independent(c, qd_s, ki_s, kt_s, gv_s):
    sl = slice(c * GLA_CHUNK, (c + 1) * GLA_CHUNK)
    out = []
    for h in range(GLA_HEADS):
        kc = slice(h * GLA_K_DIM, (h + 1) * GLA_K_DIM)
        vc = slice(h * GLA_V_DIM, (h + 1) * GLA_V_DIM)
        out.append((_dot_nt(qd_s[sl, kc], ki_s[sl, kc]), _dot_tn(kt_s[sl, kc], gv_s[sl, vc])))
    return out


def _gla_dependent(c, indep, qd_s, gv_s, sg_s, dk_s, gn_ref, st_s, y_ref):
    row = lax.broadcasted_iota(jnp.int32, (GLA_CHUNK, GLA_CHUNK), 0)
    col = lax.broadcasted_iota(jnp.int32, (GLA_CHUNK, GLA_CHUNK), 1)
    tril = col <= row
    gn = gn_ref[...]
    sl = slice(c * GLA_CHUNK, (c + 1) * GLA_CHUNK)
    for h in range(GLA_HEADS):
        kc = slice(h * GLA_K_DIM, (h + 1) * GLA_K_DIM)
        vc = slice(h * GLA_V_DIM, (h + 1) * GLA_V_DIM)
        oc = slice(ATT_WIDTH + h * GLA_V_DIM, ATT_WIDTH + (h + 1) * GLA_V_DIM)
        scores, kv_inc = indep[h]
        st = st_s[h]
        a = jnp.where(tril, scores, 0.0).astype(BF16)
        o = _dot(jnp.concatenate([qd_s[sl, kc], a], axis=1),
                 jnp.concatenate([st.astype(BF16), gv_s[sl, vc]], axis=0))
        st_s[h] = st * dk_s[kc, c:c + 1] + kv_inc
        y = _rms(o, gn) * sg_s[sl, oc].astype(F32)
        y_ref[sl, oc] = y.astype(BF16)


def _mixer_kernel(sinks_ref, x_ref, pos_ref, nmix_ref, w_ref, wup_ref, bg_ref, rope_ref, ltri_ref, gn_ref,
                  y_ref, u_s, trig_s, glow_s, lg_s, b_s, qstack_s, score_s, p_s,
                  q_s, kv_s, qd_s, ki_s, kt_s, gv_s, sg_s, dk_s, st_s, *,
                  tiles_per_seq):
    i = pl.program_id(0)
    tm = x_ref.shape[0]
    first = lax.rem(i - 1, tiles_per_seq) == 0

    @pl.when(i == 0)
    def _():
        for s in (q_s, kv_s, qd_s, ki_s, kt_s, gv_s, sg_s, dk_s):
            s[1] = jnp.zeros(s.shape[1:], s.dtype)

    @pl.when(first | (i == 0))
    def _():
        st_s[...] = jnp.zeros_like(st_s)

    def step(wr):
        rd = 1 - wr
        inproj = _inproj_pieces(
            x_ref, pos_ref, nmix_ref, w_ref, wup_ref, bg_ref, rope_ref, ltri_ref, u_s, trig_s, glow_s, lg_s, b_s,
            q_s.at[wr], kv_s.at[wr], qd_s.at[wr], ki_s.at[wr], kt_s.at[wr], gv_s.at[wr], sg_s.at[wr],
            dk_s.at[wr])
        q_r, kv_r, sg_r, gv_r, qd_r = q_s.at[rd], kv_s.at[rd], sg_s.at[rd], gv_s.at[rd], qd_s.at[rd]
        first_bias = jnp.where(first, -jnp.inf, 0.0).astype(F32)

        inproj(("norm",))
        for r, (long_group, short_group) in enumerate(_INPROJ_ROUNDS):
            j, g = divmod(r, ATT_KV_HEADS)
            _attn_scores(j, g, q_r, kv_r, qstack_s, score_s)
            sink_terms = _attn_softmax(g, first_bias if j == 0 else None, sinks_ref, score_s, p_s)
            gla_indep = _gla_independent(r, qd_r, ki_s.at[rd], kt_s.at[rd], gv_r)
            for spec in long_group:
                inproj(spec)
            _attn_values(j, g, sink_terms, p_s, kv_r, sg_r, y_ref)
            _gla_dependent(r, gla_indep, qd_r, gv_r, sg_r, dk_s.at[rd], gn_ref, st_s, y_ref)
            for spec in short_group:
                inproj(spec)
        kv_s[wr, 0:BLOCK, :] = kv_s[rd, tm:tm + BLOCK, :]

    assert len(_INPROJ_ROUNDS) == (tm // BLOCK) * ATT_KV_HEADS == tm // GLA_CHUNK
    parity = lax.rem(i, 2)
    for wr in range(2):
        pl.when(parity == wr)(functools.partial(step, wr))


def _wprep_source_row(c):
    n_head = C_GL // W_BLK
    tail_src = C_GL + GLA_GATE_RANK
    return jnp.where(c <= n_head, c * W_BLK, tail_src + (c - n_head - 1) * W_BLK)


def _wprep_kernel(*refs):
    *wt_refs, o_ref = refs
    for k, wt_ref in enumerate(wt_refs):
        c = pl.program_id(0) * WPREP_BLKS + k
        feat = _wprep_source_row(c) + lax.broadcasted_iota(jnp.int32, (W_BLK, 1), 0)
        scale = jnp.where(feat < ATT_WIDTH, ATT_HEAD_DIM ** -0.5, 1.0)
        padding = (c == C_GL // W_BLK) & (feat >= C_GL + GLA_GATE_RANK)
        o_ref[:, k * W_BLK:(k + 1) * W_BLK] = jnp.where(padding, 0.0, wt_ref[...] * scale).T.astype(BF16)


def _out_kernel(x_ref, y_ref, p_ref, wo_ref, pn_ref, wpg_ref, wpp_ref, fn_ref, o_ref, wo_s, wpg_s, wpp_s):
    @pl.when(pl.program_id(0) == 0)
    def _():
        wo_s[...] = wo_ref[...].astype(BF16)
        wpg_s[...] = wpg_ref[...].astype(BF16)
        wpp_s[...] = wpp_ref[...].astype(BF16)

    tm = x_ref.shape[0]
    halves = [slice(k * (tm // 2), (k + 1) * (tm // 2)) for k in range(2)]
    hs = [x_ref[r, :] + _dot(y_ref[r, :], wo_s[...]) for r in halves]
    ples = [_dot(p_ref[r, :].astype(BF16), wpp_s[...]) for r in halves]
    gates = [_dot(_rms(h, pn_ref[...]).astype(BF16), wpg_s[...]) for h in hs]
    for r, h, gate, ple in zip(halves, hs, gates, ples):
        o_ref[r, :] = _rms(h + _sigmoid(gate) * ple, fn_ref[...])


def _const_spec(shape):
    nd = len(shape)
    return pl.BlockSpec(shape, lambda *_: (0,) * nd, pipeline_mode=pl.Buffered(1))


def _rope_freqs():
    assert ROPE_HALF == SUBLANES
    inv_freq = ROPE_THETA ** (-jnp.arange(0, ROPE_DIM, 2, dtype=F32) / ROPE_DIM)
    return jnp.broadcast_to(inv_freq[:, None], (ROPE_HALF, LANES))


def _chunk_tril(n):
    i = np.arange(n)
    same = (i[:, None] // GLA_CHUNK) == (i[None, :] // GLA_CHUNK)
    return (same & (i[None, :] <= i[:, None])).astype(np.float32)


def _layer(h, p, pos, norm_mix, w_in, sinks, w_gate_up, b_gate, gla_norm, w_out, ple_norm,
           w_ple_gate, w_ple_proj, out_norm, batch, seq):
    n = batch * seq
    assert w_in.shape == (D_MODEL, D_IN_PROJ)
    w_t = jnp.swapaxes(w_in, 0, 1)
    src_spec = lambda k: pl.BlockSpec(
        (pl.Element(W_BLK), pl.Element(D_MODEL)),
        lambda c: (pl.multiple_of(_wprep_source_row(c * WPREP_BLKS + k), SUBLANES), 0))
    w_r = pl.pallas_call(
        _wprep_kernel,
        grid=(C_END // (WPREP_BLKS * W_BLK),),
        in_specs=[src_spec(k) for k in range(WPREP_BLKS)],
        out_specs=pl.BlockSpec((D_MODEL, WPREP_BLKS * W_BLK), lambda c: (0, c)),
        out_shape=jax.ShapeDtypeStruct((D_MODEL, C_END), BF16),
        compiler_params=pltpu.CompilerParams(dimension_semantics=("parallel",)),
        name="wprep",
    )(*([w_t] * WPREP_BLKS))
    wup = jnp.concatenate([w_gate_up, jnp.zeros((LANES - GLA_GATE_RANK, GLA_QK_WIDTH), w_gate_up.dtype)],
                          axis=0).astype(BF16)
    rope_freqs = _rope_freqs()
    ltri = jnp.asarray(_chunk_tril(CUM_BLK), dtype=BF16)

    params = lambda sem: pltpu.CompilerParams(dimension_semantics=sem, vmem_limit_bytes=VMEM_LIMIT)

    nt = n // TM_MIX
    in_tile = lambda i: (jnp.minimum(i, nt - 1), 0)
    out_tile = lambda i: (jnp.maximum(i - 1, 0), 0)
    two = lambda *shape: (2,) + shape
    y = pl.pallas_call(
        functools.partial(_mixer_kernel, tiles_per_seq=seq // TM_MIX),
        grid=(nt + 1,),
        in_specs=[
            pl.BlockSpec(memory_space=pltpu.SMEM),
            pl.BlockSpec((TM_MIX, D_MODEL), in_tile),
            pl.BlockSpec((1, TM_MIX // LANES, LANES), lambda i: (jnp.minimum(i, nt - 1), 0, 0)),
            _const_spec((1, D_MODEL)),
            _const_spec((D_MODEL, C_END)),
            _const_spec((LANES, GLA_QK_WIDTH)),
            _const_spec((1, GLA_QK_WIDTH)),
            _const_spec((SUBLANES, LANES)),
            _const_spec((CUM_BLK, CUM_BLK)),
            _const_spec((1, GLA_V_DIM)),
        ],
        out_specs=pl.BlockSpec((TM_MIX, ATT_WIDTH + GLA_WIDTH), out_tile),
        out_shape=jax.ShapeDtypeStruct((n, ATT_WIDTH + GLA_WIDTH), BF16),
        scratch_shapes=[
            pltpu.VMEM((TM_MIX, D_MODEL), BF16),
            pltpu.VMEM((3, TM_MIX, LANES), F32),
            pltpu.VMEM((TM_MIX, LANES), BF16),
            pltpu.VMEM((2, TM_MIX, GLA_QK_WIDTH), BF16),
            pltpu.VMEM((2, TM_MIX, GLA_QK_WIDTH), F32),
            pltpu.VMEM((ATT_GROUP * BLOCK, LANES), BF16),
            pltpu.VMEM((ATT_GROUP * BLOCK, 2 * BLOCK), F32),
            pltpu.VMEM((ATT_GROUP * BLOCK, 2 * BLOCK), BF16),
            pltpu.VMEM(two(TM_MIX, ATT_WIDTH), BF16),
            pltpu.VMEM(two(TM_MIX + BLOCK, 4 * ATT_KV_WIDTH), BF16),
            pltpu.VMEM(two(TM_MIX, GLA_QK_WIDTH), BF16),
            pltpu.VMEM(two(TM_MIX, GLA_QK_WIDTH), BF16),
            pltpu.VMEM(two(TM_MIX, GLA_QK_WIDTH), BF16),
            pltpu.VMEM(two(TM_MIX, GLA_WIDTH), BF16),
            pltpu.VMEM(two(TM_MIX, ATT_WIDTH + GLA_WIDTH), BF16),
            pltpu.VMEM(two(GLA_QK_WIDTH, LANES), F32),
            pltpu.VMEM((GLA_HEADS, GLA_K_DIM, GLA_V_DIM), F32),
        ],
        compiler_params=params(("arbitrary",)),
        name="mixer",
    )(sinks.astype(F32), h, pos.reshape(nt, TM_MIX // LANES, LANES), norm_mix.reshape(1, D_MODEL), w_r, wup,
      b_gate.reshape(1, GLA_QK_WIDTH), rope_freqs, ltri, gla_norm.reshape(1, GLA_V_DIM))

    row = lambda i: (i, 0)
    return pl.pallas_call(
        _out_kernel,
        grid=(n // TM_OUT,),
        in_specs=[
            pl.BlockSpec((TM_OUT, D_MODEL), row),
            pl.BlockSpec((TM_OUT, ATT_WIDTH + GLA_WIDTH), row),
            pl.BlockSpec((TM_OUT, D_PLE), row),
            _const_spec((ATT_WIDTH + GLA_WIDTH, D_MODEL)),
            _const_spec((1, D_MODEL)),
            _const_spec((D_MODEL, D_MODEL)),
            _const_spec((D_PLE, D_MODEL)),
            _const_spec((1, D_MODEL)),
        ],
        out_specs=pl.BlockSpec((TM_OUT, D_MODEL), row),
        out_shape=jax.ShapeDtypeStruct((n, D_MODEL), F32),
        scratch_shapes=[
            pltpu.VMEM((ATT_WIDTH + GLA_WIDTH, D_MODEL), BF16),
            pltpu.VMEM((D_MODEL, D_MODEL), BF16),
            pltpu.VMEM((D_PLE, D_MODEL), BF16),
        ],
        compiler_params=params(("arbitrary",)),
        name="outproj",
    )(h, y, p, w_out, ple_norm.reshape(1, D_MODEL), w_ple_gate, w_ple_proj, out_norm.reshape(1, D_MODEL))


def kernel(x, p, positions, norm_mix, w_in, attn_sinks, w_gate_up, b_gate, gla_norm, w_out, ple_norm,
           w_ple_gate, w_ple_proj, final_norm):
    batch, seq, _ = x.shape
    depth = w_in.shape[0]
    assert depth == 1, "the final RMSNorm is fused into the single layer's output kernel"
    n = batch * seq
    layer = lambda a: a.reshape(a.shape[1:])
    out = _layer(x.reshape(n, D_MODEL), p.reshape(n, D_PLE), positions.reshape(n), layer(norm_mix), layer(w_in),
                 layer(attn_sinks), layer(w_gate_up), layer(b_gate), layer(gla_norm), layer(w_out),
                 layer(ple_norm), layer(w_ple_gate), layer(w_ple_proj), final_norm, batch, seq)
    return out.reshape(batch, seq, D_MODEL)
```

```python
import functools

import jax
import jax.numpy as jnp
import numpy as np
from jax import lax
from jax.experimental import pallas as pl
from jax.experimental.pallas import tpu as pltpu

D_MODEL = 1024
D_PLE = 256
ATT_HEADS = 16
ATT_KV_HEADS = 2
ATT_HEAD_DIM = 64
ATT_WIDTH = ATT_HEADS * ATT_HEAD_DIM
ATT_KV_WIDTH = ATT_KV_HEADS * ATT_HEAD_DIM
ATT_GROUP = ATT_HEADS // ATT_KV_HEADS
BLOCK = 128
ROPE_DIM = ATT_HEAD_DIM // 4
ROPE_HALF = ROPE_DIM // 2
ROPE_THETA = 500000.0
GLA_HEADS = 4
GLA_V_DIM = 256
GLA_K_DIM = 128
GLA_WIDTH = GLA_HEADS * GLA_V_DIM
GLA_QK_WIDTH = GLA_HEADS * GLA_K_DIM
GLA_GATE_RANK = 16
GLA_GATE_NORMALIZER = 16.0
GLA_CHUNK = 64
EPS = 1e-6

LANES = 128
SUBLANES = 8
VMEM_LIMIT = 56 * 1024 * 1024

W_BLK = 256
WPREP_BLKS = 11

C_Q = 0
C_KV = C_Q + ATT_WIDTH
C_GQ = C_KV + 2 * ATT_KV_WIDTH
C_GK = C_GQ + GLA_QK_WIDTH
C_GV = C_GK + GLA_QK_WIDTH
C_GL = C_GV + GLA_WIDTH
C_ZA = C_GL + W_BLK
C_ZG = C_ZA + ATT_WIDTH
C_END = C_ZG + GLA_WIDTH
D_IN_PROJ = C_END - W_BLK + GLA_GATE_RANK

TM_MIX = 256
TM_OUT = 512
CUM_BLK = 256

BF16 = jnp.bfloat16
F32 = jnp.float32


def _dot(a, b):
    return jnp.dot(a, b, preferred_element_type=F32)


def _dot_nt(a, b):
    return lax.dot_general(a, b, (((1,), (1,)), ((), ())), preferred_element_type=F32)


def _dot_tn(a, b):
    return lax.dot_general(a, b, (((0,), (0,)), ((), ())), preferred_element_type=F32)


def _rms(x, g):
    return x * lax.rsqrt(jnp.mean(x * x, axis=-1, keepdims=True) + EPS) * g


def _sigmoid(z):
    return 0.5 * (1.0 + jnp.tanh(0.5 * z))


def _inproj_pieces(x_ref, pos_ref, nmix_ref, w_ref, wup_ref, bg_ref, rope_ref, ltri_ref,
                   u_s, trig_s, glow_s, lg_s, b_s, q_s, kv_s, qd_s, ki_s, kt_s, gv_s, sg_s, dk_s):
    tm = x_ref.shape[0]

    def proj(c0, width):
        return _dot(u_s[...], w_ref[:, c0:c0 + width])

    def norm():
        u_s[...] = _rms(x_ref[...], nmix_ref[...]).astype(BF16)
        pos = pos_ref[0].astype(F32)
        ones = jnp.ones((ATT_HEAD_DIM - ROPE_DIM, LANES), F32)
        zeros = jnp.zeros((ATT_HEAD_DIM - ROPE_DIM, LANES), F32)
        zero8 = jnp.zeros((ROPE_HALF, LANES), F32)
        for r in range(tm // LANES):
            ang = rope_ref[...] * pos[r:r + 1, :]
            c8 = jnp.cos(ang)
            s8 = jnp.sin(ang)
            rows = slice(r * LANES, (r + 1) * LANES)
            trig_s[0, rows, :] = jnp.concatenate([c8, c8, ones] * 2, axis=0).T
            trig_s[1, rows, :] = jnp.concatenate([-s8, zero8, zeros] * 2, axis=0).T
            trig_s[2, rows, :] = jnp.concatenate([zero8, s8, zeros] * 2, axis=0).T

    def rope(z):
        up = pltpu.roll(z, LANES - ROPE_HALF, 1)
        dn = pltpu.roll(z, ROPE_HALF, 1)
        return z * trig_s[0] + up * trig_s[1] + dn * trig_s[2]

    def q_piece(lo, width):
        z = proj(C_Q + lo, width)
        for l in range(0, width, LANES):
            q_s[:, lo + l:lo + l + LANES] = rope(z[:, l:l + LANES]).astype(BF16)

    def kv_piece(lo, width):
        assert (lo, width) == (0, 2 * ATT_KV_WIDTH)
        z = proj(C_KV, width)
        lo_half = lax.broadcasted_iota(jnp.int32, (tm, LANES), 1) < ATT_HEAD_DIM
        k = rope(z[:, :LANES])
        k_sw = pltpu.roll(k, ATT_HEAD_DIM, 1)
        v = z[:, LANES:]
        v_sw = pltpu.roll(v, ATT_HEAD_DIM, 1)
        kv_s[BLOCK:, 0 * LANES:1 * LANES] = jnp.where(lo_half, k, k_sw).astype(BF16)
        kv_s[BLOCK:, 1 * LANES:2 * LANES] = jnp.where(lo_half, k_sw, k).astype(BF16)
        kv_s[BLOCK:, 2 * LANES:3 * LANES] = jnp.where(lo_half, v, 1.0).astype(BF16)
        kv_s[BLOCK:, 3 * LANES:4 * LANES] = jnp.where(lo_half, v_sw, 1.0).astype(BF16)

    def gv_piece(lo, width):
        gv_s[:, lo:lo + width] = proj(C_GV + lo, width).astype(BF16)

    def sg_piece(lo, width):
        z = proj(C_ZA + lo, width)
        sg_s[:, lo:lo + width] = (z * _sigmoid(z)).astype(BF16)

    def gate_low():
        glow_s[...] = proj(C_GL, LANES).astype(BF16)

    def gate_logsig():
        logit = _dot(glow_s[...], wup_ref[...]) + bg_ref[...]
        lg = (jnp.minimum(logit, 0.0) - jnp.log(1.0 + jnp.exp(-jnp.abs(logit)))) * (1.0 / GLA_GATE_NORMALIZER)
        lg_hi = lg.astype(BF16)
        lg_s[0] = lg_hi
        lg_s[1] = (lg - lg_hi.astype(F32)).astype(BF16)

    def gate_cumsum():
        ltri = ltri_ref[...]
        nb = CUM_BLK // GLA_CHUNK
        chunk_totals = []
        for r in range(0, tm, CUM_BLK):
            b = _dot(ltri, lg_s[0, r:r + CUM_BLK, :]) + _dot(ltri, lg_s[1, r:r + CUM_BLK, :])
            b3 = b.reshape(nb, GLA_CHUNK, GLA_QK_WIDTH)
            b_last = b3[:, GLA_CHUNK - 1:GLA_CHUNK, :]
            b_s[0, r:r + CUM_BLK, :] = b
            b_s[1, r:r + CUM_BLK, :] = (b_last - b3).reshape(CUM_BLK, GLA_QK_WIDTH)
            chunk_totals.append(b_last.reshape(nb, GLA_QK_WIDTH))
        pad = jnp.zeros((LANES - tm // GLA_CHUNK, GLA_QK_WIDTH), F32)
        decay = jnp.exp(jnp.concatenate(chunk_totals + [pad], axis=0))
        for h in range(GLA_HEADS):
            kc = slice(h * GLA_K_DIM, (h + 1) * GLA_K_DIM)
            dk_s[kc, :] = decay[:, kc].T

    def gq_piece(lo, width):
        z = proj(C_GQ + lo, width) * (GLA_K_DIM ** -0.5)
        qd_s[:, lo:lo + width] = (z * jnp.exp(b_s[0, :, lo:lo + width])).astype(BF16)

    def gk_piece(lo, width):
        z = proj(C_GK + lo, width)
        ki_s[:, lo:lo + width] = (z * jnp.exp(-b_s[0, :, lo:lo + width])).astype(BF16)
        kt_s[:, lo:lo + width] = (z * jnp.exp(b_s[1, :, lo:lo + width])).astype(BF16)

    column_pieces = {"q": q_piece, "kv": kv_piece, "gv": gv_piece, "sg": sg_piece, "gq": gq_piece, "gk": gk_piece}
    single_pieces = {"norm": norm, "gate_low": gate_low, "gate_logsig": gate_logsig, "gate_cumsum": gate_cumsum}

    def run(spec):
        if spec[0] in single_pieces:
            single_pieces[spec[0]]()
        else:
            column_pieces[spec[0]](spec[1], spec[2])
    return run


_INPROJ_ROUNDS = (
    ((("q", 0, 512), ("q", 512, 512)), (("kv", 0, 256), ("gate_low",))),
    ((("gv", 0, 512), ("gv", 512, 512)), (("gate_logsig",),)),
    ((("sg", 0, 512), ("sg", 512, 512)), (("gate_cumsum",),)),
    ((("sg", 1024, 512), ("gq", 0, 512)), (("sg", 1536, 512), ("gk", 0, 512))),
)


def _attn_scores(j, g, q_s, kv_s, qstack_s, score_s):
    col = lax.broadcasted_iota(jnp.int32, (BLOCK, BLOCK), 1)
    lo_half = col < ATT_HEAD_DIM
    zero = jnp.zeros((), BF16)
    kcat = kv_s[j * BLOCK:(j + 2) * BLOCK, g * LANES:(g + 1) * LANES]
    for pr in range(ATT_GROUP // 2):
        c0 = (g * (ATT_GROUP // 2) + pr) * LANES
        qp = q_s[j * BLOCK:(j + 1) * BLOCK, c0:c0 + LANES]
        qstack_s[(2 * pr) * BLOCK:(2 * pr + 1) * BLOCK, :] = jnp.where(lo_half, qp, zero)
        qstack_s[(2 * pr + 1) * BLOCK:(2 * pr + 2) * BLOCK, :] = jnp.where(lo_half, zero, qp)
    score_s[...] = _dot_nt(qstack_s[...], kcat)


def _attn_softmax(g, prev_bias, sinks_ref, score_s, p_s):
    row = lax.broadcasted_iota(jnp.int32, (BLOCK, BLOCK), 0)
    col = lax.broadcasted_iota(jnp.int32, (BLOCK, BLOCK), 1)
    causal = col <= row
    sink_terms = []
    for h in range(ATT_GROUP):
        rows = slice(h * BLOCK, (h + 1) * BLOCK)
        s_prev = score_s[rows, :BLOCK]
        if prev_bias is not None:
            s_prev = s_prev + prev_bias
        sf = jnp.where(causal, score_s[rows, BLOCK:], s_prev)
        sink = sinks_ref[g * ATT_GROUP + h]
        m = jnp.maximum(jnp.max(sf, axis=-1, keepdims=True), sink)
        e = jnp.exp(sf - m)
        sink_terms.append(jnp.exp(sink - m))
        p_s[rows, :BLOCK] = jnp.where(causal, 0.0, e).astype(BF16)
        p_s[rows, BLOCK:] = jnp.where(causal, e, 0.0).astype(BF16)
    return sink_terms


def _attn_values(j, g, sink_terms, p_s, kv_s, sg_s, y_ref):
    col = lax.broadcasted_iota(jnp.int32, (BLOCK, BLOCK), 1)
    lo_half = col < ATT_HEAD_DIM
    rows = slice(j * BLOCK, (j + 1) * BLOCK)
    vcat = kv_s[j * BLOCK:(j + 2) * BLOCK, (2 + g) * LANES:(3 + g) * LANES]
    o = _dot(p_s[...], vcat)
    for pr in range(ATT_GROUP // 2):
        he, ho = 2 * pr, 2 * pr + 1
        oe = o[he * BLOCK:(he + 1) * BLOCK]
        oo = o[ho * BLOCK:(ho + 1) * BLOCK]
        num = jnp.where(lo_half, oe, pltpu.roll(oo, ATT_HEAD_DIM, 1))
        den = jnp.where(lo_half, pltpu.roll(oe, ATT_HEAD_DIM, 1), oo) \
            + jnp.where(lo_half, sink_terms[he], sink_terms[ho])
        c0 = (g * (ATT_GROUP // 2) + pr) * LANES
        y = num * (1.0 / den) * sg_s[rows, c0:c0 + LANES].astype(F32)
        y_ref[rows, c0:c0 + LANES] = y.astype(BF16)


def _gla_independent(c, qd_s, ki_s, kt_s, gv_s):
    sl = slice(c * GLA_CHUNK, (c + 1) * GLA_CHUNK)
    out = []
    for h in range(GLA_HEADS):
        kc = slice(h * GLA_K_DIM, (h + 1) * GLA_K_DIM)
        vc = slice(h * GLA_V_DIM, (h + 1) * GLA_V_DIM)
        out.append((_dot_nt(qd_s[sl, kc], ki_s[sl, kc]), _dot_tn(kt_s[sl, kc], gv_s[sl, vc])))
    return out


def _gla_dependent(c, indep, qd_s, gv_s, sg_s, dk_s, gn_ref, st_s, y_ref):
    row = lax.broadcasted_iota(jnp.int32, (GLA_CHUNK, GLA_CHUNK), 0)
    col = lax.broadcasted_iota(jnp.int32, (GLA_CHUNK, GLA_CHUNK), 1)
    tril = col <= row
    gn = gn_ref[...]
    sl = slice(c * GLA_CHUNK, (c + 1) * GLA_CHUNK)
    for h in range(GLA_HEADS):
        kc = slice(h * GLA_K_DIM, (h + 1) * GLA_K_DIM)
        vc = slice(h * GLA_V_DIM, (h + 1) * GLA_V_DIM)
        oc = slice(ATT_WIDTH + h * GLA_V_DIM, ATT_WIDTH + (h + 1) * GLA_V_DIM)
        scores, kv_inc = indep[h]
        st = st_s[h]
        a = jnp.where(tril, scores, 0.0).astype(BF16)
        o = _dot(jnp.concatenate([qd_s[sl, kc], a], axis=1),
                 jnp.concatenate([st.astype(BF16), gv_s[sl, vc]], axis=0))
        st_s[h] = st * dk_s[kc, c:c + 1] + kv_inc
        y = _rms(o, gn) * sg_s[sl, oc].astype(F32)
        y_ref[sl, oc] = y.astype(BF16)


def _mixer_kernel(sinks_ref, x_ref, pos_ref, nmix_ref, w_ref, wup_ref, bg_ref, rope_ref, ltri_ref, gn_ref,
                  y_ref, u_s, trig_s, glow_s, lg_s, b_s, qstack_s, score_s, p_s,
                  q_s, kv_s, qd_s, ki_s, kt_s, gv_s, sg_s, dk_s, st_s, *,
                  tiles_per_seq):
    i = pl.program_id(0)
    tm = x_ref.shape[0]
    first = lax.rem(i - 1, tiles_per_seq) == 0

    @pl.when(i == 0)
    def _():
        for s in (q_s, kv_s, qd_s, ki_s, kt_s, gv_s, sg_s, dk_s):
            s[1] = jnp.zeros(s.shape[1:], s.dtype)

    @pl.when(first | (i == 0))
    def _():
        st_s[...] = jnp.zeros_like(st_s)

    def step(wr):
        rd = 1 - wr
        inproj = _inproj_pieces(
            x_ref, pos_ref, nmix_ref, w_ref, wup_ref, bg_ref, rope_ref, ltri_ref, u_s, trig_s, glow_s, lg_s, b_s,
            q_s.at[wr], kv_s.at[wr], qd_s.at[wr], ki_s.at[wr], kt_s.at[wr], gv_s.at[wr], sg_s.at[wr],
            dk_s.at[wr])
        q_r, kv_r, sg_r, gv_r, qd_r = q_s.at[rd], kv_s.at[rd], sg_s.at[rd], gv_s.at[rd], qd_s.at[rd]
        first_bias = jnp.where(first, -jnp.inf, 0.0).astype(F32)

        inproj(("norm",))
        for r, (long_group, short_group) in enumerate(_INPROJ_ROUNDS):
            j, g = divmod(r, ATT_KV_HEADS)
            _attn_scores(j, g, q_r, kv_r, qstack_s, score_s)
            sink_terms = _attn_softmax(g, first_bias if j == 0 else None, sinks_ref, score_s, p_s)
            gla_indep = _gla_independent(r, qd_r, ki_s.at[rd], kt_s.at[rd], gv_r)
            for spec in long_group:
                inproj(spec)
            _attn_values(j, g, sink_terms, p_s, kv_r, sg_r, y_ref)
            _gla_dependent(r, gla_indep, qd_r, gv_r, sg_r, dk_s.at[rd], gn_ref, st_s, y_ref)
            for spec in short_group:
                inproj(spec)
        kv_s[wr, 0:BLOCK, :] = kv_s[rd, tm:tm + BLOCK, :]

    assert len(_INPROJ_ROUNDS) == (tm // BLOCK) * ATT_KV_HEADS == tm // GLA_CHUNK
    parity = lax.rem(i, 2)
    for wr in range(2):
        pl.when(parity == wr)(functools.partial(step, wr))


def _wprep_source_row(c):
    n_head = C_GL // W_BLK
    tail_src = C_GL + GLA_GATE_RANK
    return jnp.where(c <= n_head, c * W_BLK, tail_src + (c - n_head - 1) * W_BLK)


def _wprep_kernel(*refs):
    *wt_refs, o_ref = refs
    for k, wt_ref in enumerate(wt_refs):
        c = pl.program_id(0) * WPREP_BLKS + k
        feat = _wprep_source_row(c) + lax.broadcasted_iota(jnp.int32, (W_BLK, 1), 0)
        scale = jnp.where(feat < ATT_WIDTH, ATT_HEAD_DIM ** -0.5, 1.0)
        padding = (c == C_GL // W_BLK) & (feat >= C_GL + GLA_GATE_RANK)
        o_ref[:, k * W_BLK:(k + 1) * W_BLK] = jnp.where(padding, 0.0, wt_ref[...] * scale).T.astype(BF16)


def _out_kernel(x_ref, y_ref, p_ref, wo_ref, pn_ref, wpg_ref, wpp_ref, fn_ref, o_ref, wo_s, wpg_s, wpp_s):
    @pl.when(pl.program_id(0) == 0)
    def _():
        wo_s[...] = wo_ref[...].astype(BF16)
        wpg_s[...] = wpg_ref[...].astype(BF16)
        wpp_s[...] = wpp_ref[...].astype(BF16)

    tm = x_ref.shape[0]
    halves = [slice(k * (tm // 2), (k + 1) * (tm // 2)) for k in range(2)]
    hs = [x_ref[r, :] + _dot(y_ref[r, :], wo_s[...]) for r in halves]
    ples = [_dot(p_ref[r, :].astype(BF16), wpp_s[...]) for r in halves]
    gates = [_dot(_rms(h, pn_ref[...]).astype(BF16), wpg_s[...]) for h in hs]
    for r, h, gate, ple in zip(halves, hs, gates, ples):
        o_ref[r, :] = _rms(h + _sigmoid(gate) * ple, fn_ref[...])


def _const_spec(shape):
    nd = len(shape)
    return pl.BlockSpec(shape, lambda *_: (0,) * nd, pipeline_mode=pl.Buffered(1))


def _rope_freqs():
    assert ROPE_HALF == SUBLANES
    inv_freq = ROPE_THETA ** (-jnp.arange(0, ROPE_DIM, 2, dtype=F32) / ROPE_DIM)
    return jnp.broadcast_to(inv_freq[:, None], (ROPE_HALF, LANES))


def _chunk_tril(n):
    i = np.arange(n)
    same = (i[:, None] // GLA_CHUNK) == (i[None, :] // GLA_CHUNK)
    return (same & (i[None, :] <= i[:, None])).astype(np.float32)


def _layer(h, p, pos, norm_mix, w_in, sinks, w_gate_up, b_gate, gla_norm, w_out, ple_norm,
           w_ple_gate, w_ple_proj, out_norm, batch, seq):
    n = batch * seq
    assert w_in.shape == (D_MODEL, D_IN_PROJ)
    w_t = jnp.swapaxes(w_in, 0, 1)
    src_spec = lambda k: pl.BlockSpec(
        (pl.Element(W_BLK), pl.Element(D_MODEL)),
        lambda c: (pl.multiple_of(_wprep_source_row(c * WPREP_BLKS + k), SUBLANES), 0))
    w_r = pl.pallas_call(
        _wprep_kernel,
        grid=(C_END // (WPREP_BLKS * W_BLK),),
        in_specs=[src_spec(k) for k in range(WPREP_BLKS)],
        out_specs=pl.BlockSpec((D_MODEL, WPREP_BLKS * W_BLK), lambda c: (0, c)),
        out_shape=jax.ShapeDtypeStruct((D_MODEL, C_END), BF16),
        compiler_params=pltpu.CompilerParams(dimension_semantics=("parallel",)),
        name="wprep",
    )(*([w_t] * WPREP_BLKS))
    wup = jnp.concatenate([w_gate_up, jnp.zeros((LANES - GLA_GATE_RANK, GLA_QK_WIDTH), w_gate_up.dtype)],
                          axis=0).astype(BF16)
    rope_freqs = _rope_freqs()
    ltri = jnp.asarray(_chunk_tril(CUM_BLK), dtype=BF16)

    params = lambda sem: pltpu.CompilerParams(dimension_semantics=sem, vmem_limit_bytes=VMEM_LIMIT)

    nt = n // TM_MIX
    in_tile = lambda i: (jnp.minimum(i, nt - 1), 0)
    out_tile = lambda i: (jnp.maximum(i - 1, 0), 0)
    two = lambda *shape: (2,) + shape
    y = pl.pallas_call(
        functools.partial(_mixer_kernel, tiles_per_seq=seq // TM_MIX),
        grid=(nt + 1,),
        in_specs=[
            pl.BlockSpec(memory_space=pltpu.SMEM),
            pl.BlockSpec((TM_MIX, D_MODEL), in_tile),
            pl.BlockSpec((1, TM_MIX // LANES, LANES), lambda i: (jnp.minimum(i, nt - 1), 0, 0)),
            _const_spec((1, D_MODEL)),
            _const_spec((D_MODEL, C_END)),
            _const_spec((LANES, GLA_QK_WIDTH)),
            _const_spec((1, GLA_QK_WIDTH)),
            _const_spec((SUBLANES, LANES)),
            _const_spec((CUM_BLK, CUM_BLK)),
            _const_spec((1, GLA_V_DIM)),
        ],
        out_specs=pl.BlockSpec((TM_MIX, ATT_WIDTH + GLA_WIDTH), out_tile),
        out_shape=jax.ShapeDtypeStruct((n, ATT_WIDTH + GLA_WIDTH), BF16),
        scratch_shapes=[
            pltpu.VMEM((TM_MIX, D_MODEL), BF16),
            pltpu.VMEM((3, TM_MIX, LANES), F32),
            pltpu.VMEM((TM_MIX, LANES), BF16),
            pltpu.VMEM((2, TM_MIX, GLA_QK_WIDTH), BF16),
            pltpu.VMEM((2, TM_MIX, GLA_QK_WIDTH), F32),
            pltpu.VMEM((ATT_GROUP * BLOCK, LANES), BF16),
            pltpu.VMEM((ATT_GROUP * BLOCK, 2 * BLOCK), F32),
            pltpu.VMEM((ATT_GROUP * BLOCK, 2 * BLOCK), BF16),
            pltpu.VMEM(two(TM_MIX, ATT_WIDTH), BF16),
            pltpu.VMEM(two(TM_MIX + BLOCK, 4 * ATT_KV_WIDTH), BF16),
            pltpu.VMEM(two(TM_MIX, GLA_QK_WIDTH), BF16),
            pltpu.VMEM(two(TM_MIX, GLA_QK_WIDTH), BF16),
            pltpu.VMEM(two(TM_MIX, GLA_QK_WIDTH), BF16),
            pltpu.VMEM(two(TM_MIX, GLA_WIDTH), BF16),
            pltpu.VMEM(two(TM_MIX, ATT_WIDTH + GLA_WIDTH), BF16),
            pltpu.VMEM(two(GLA_QK_WIDTH, LANES), F32),
            pltpu.VMEM((GLA_HEADS, GLA_K_DIM, GLA_V_DIM), F32),
        ],
        compiler_params=params(("arbitrary",)),
        name="mixer",
    )(sinks.astype(F32), h, pos.reshape(nt, TM_MIX // LANES, LANES), norm_mix.reshape(1, D_MODEL), w_r, wup,
      b_gate.reshape(1, GLA_QK_WIDTH), rope_freqs, ltri, gla_norm.reshape(1, GLA_V_DIM))

    row = lambda i: (i, 0)
    return pl.pallas_call(
        _out_kernel,
        grid=(n // TM_OUT,),
        in_specs=[
            pl.BlockSpec((TM_OUT, D_MODEL), row),
            pl.BlockSpec((TM_OUT, ATT_WIDTH + GLA_WIDTH), row),
            pl.BlockSpec((TM_OUT, D_PLE), row),
            _const_spec((ATT_WIDTH + GLA_WIDTH, D_MODEL)),
            _const_spec((1, D_MODEL)),
            _const_spec((D_MODEL, D_MODEL)),
            _const_spec((D_PLE, D_MODEL)),
            _const_spec((1, D_MODEL)),
        ],
        out_specs=pl.BlockSpec((TM_OUT, D_MODEL), row),
        out_shape=jax.ShapeDtypeStruct((n, D_MODEL), F32),
        scratch_shapes=[
            pltpu.VMEM((ATT_WIDTH + GLA_WIDTH, D_MODEL), BF16),
            pltpu.VMEM((D_MODEL, D_MODEL), BF16),
            pltpu.VMEM((D_PLE, D_MODEL), BF16),
        ],
        compiler_params=params(("arbitrary",)),
        name="outproj",
    )(h, y, p, w_out, ple_norm.reshape(1, D_MODEL), w_ple_gate, w_ple_proj, out_norm.reshape(1, D_MODEL))


def kernel(x, p, positions, norm_mix, w_in, attn_sinks, w_gate_up, b_gate, gla_norm, w_out, ple_norm,
           w_ple_gate, w_ple_proj, final_norm):
    batch, seq, _ = x.shape
    depth = w_in.shape[0]
    assert depth == 1, "the final RMSNorm is fused into the single layer's output kernel"
    n = batch * seq
    layer = lambda a: a.reshape(a.shape[1:])
    out = _layer(x.reshape(n, D_MODEL), p.reshape(n, D_PLE), positions.reshape(n), layer(norm_mix), layer(w_in),
                 layer(attn_sinks), layer(w_gate_up), layer(b_gate), layer(gla_norm), layer(w_out),
                 layer(ple_norm), layer(w_ple_gate), layer(w_ple_proj), final_norm, batch, seq)
    return out.reshape(batch, seq, D_MODEL)
```

```python
import functools

import jax
import jax.numpy as jnp
import numpy as np
from jax import lax
from jax.experimental import pallas as pl
from jax.experimental.pallas import tpu as pltpu

D_MODEL = 1024
D_PLE = 256
ATT_HEADS = 16
ATT_KV_HEADS = 2
ATT_HEAD_DIM = 64
ATT_WIDTH = ATT_HEADS * ATT_HEAD_DIM
ATT_KV_WIDTH = ATT_KV_HEADS * ATT_HEAD_DIM
ATT_GROUP = ATT_HEADS // ATT_KV_HEADS
BLOCK = 128
ROPE_DIM = ATT_HEAD_DIM // 4
ROPE_HALF = ROPE_DIM // 2
ROPE_THETA = 500000.0
GLA_HEADS = 4
GLA_V_DIM = 256
GLA_K_DIM = 128
GLA_WIDTH = GLA_HEADS * GLA_V_DIM
GLA_QK_WIDTH = GLA_HEADS * GLA_K_DIM
GLA_GATE_RANK = 16
GLA_GATE_NORMALIZER = 16.0
GLA_CHUNK = 64
EPS = 1e-6

LANES = 128
SUBLANES = 8
VMEM_LIMIT = 56 * 1024 * 1024

W_BLK = 256
WPREP_BLKS = 11

C_Q = 0
C_KV = C_Q + ATT_WIDTH
C_GQ = C_KV + 2 * ATT_KV_WIDTH
C_GK = C_GQ + GLA_QK_WIDTH
C_GV = C_GK + GLA_QK_WIDTH
C_GL = C_GV + GLA_WIDTH
C_ZA = C_GL + W_BLK
C_ZG = C_ZA + ATT_WIDTH
C_END = C_ZG + GLA_WIDTH
D_IN_PROJ = C_END - W_BLK + GLA_GATE_RANK

TM_MIX = 256
TM_OUT = 512
CUM_BLK = 256

BF16 = jnp.bfloat16
F32 = jnp.float32


def _dot(a, b):
    return jnp.dot(a, b, preferred_element_type=F32)


def _dot_nt(a, b):
    return lax.dot_general(a, b, (((1,), (1,)), ((), ())), preferred_element_type=F32)


def _dot_tn(a, b):
    return lax.dot_general(a, b, (((0,), (0,)), ((), ())), preferred_element_type=F32)


def _rms(x, g):
    return x * lax.rsqrt(jnp.mean(x * x, axis=-1, keepdims=True) + EPS) * g


def _sigmoid(z):
    return 0.5 * (1.0 + jnp.tanh(0.5 * z))


def _inproj_pieces(x_ref, pos_ref, nmix_ref, w_ref, wup_ref, bg_ref, rope_ref, ltri_ref,
                   u_s, trig_s, glow_s, lg_s, b_s, q_s, kv_s, qd_s, ki_s, kt_s, gv_s, sg_s, dk_s):
    tm = x_ref.shape[0]

    def proj(c0, width):
        return _dot(u_s[...], w_ref[:, c0:c0 + width])

    def norm():
        u_s[...] = _rms(x_ref[...], nmix_ref[...]).astype(BF16)
        pos = pos_ref[0].astype(F32)
        ones = jnp.ones((ATT_HEAD_DIM - ROPE_DIM, LANES), F32)
        zeros = jnp.zeros((ATT_HEAD_DIM - ROPE_DIM, LANES), F32)
        zero8 = jnp.zeros((ROPE_HALF, LANES), F32)
        for r in range(tm // LANES):
            ang = rope_ref[...] * pos[r:r + 1, :]
            c8 = jnp.cos(ang)
            s8 = jnp.sin(ang)
            rows = slice(r * LANES, (r + 1) * LANES)
            trig_s[0, rows, :] = jnp.concatenate([c8, c8, ones] * 2, axis=0).T
            trig_s[1, rows, :] = jnp.concatenate([-s8, zero8, zeros] * 2, axis=0).T
            trig_s[2, rows, :] = jnp.concatenate([zero8, s8, zeros] * 2, axis=0).T

    def rope(z):
        up = pltpu.roll(z, LANES - ROPE_HALF, 1)
        dn = pltpu.roll(z, ROPE_HALF, 1)
        return z * trig_s[0] + up * trig_s[1] + dn * trig_s[2]

    def q_piece(lo, width):
        z = proj(C_Q + lo, width)
        for l in range(0, width, LANES):
            q_s[:, lo + l:lo + l + LANES] = rope(z[:, l:l + LANES]).astype(BF16)

    def kv_piece(lo, width):
        assert (lo, width) == (0, 2 * ATT_KV_WIDTH)
        z = proj(C_KV, width)
        lo_half = lax.broadcasted_iota(jnp.int32, (tm, LANES), 1) < ATT_HEAD_DIM
        k = rope(z[:, :LANES])
        k_sw = pltpu.roll(k, ATT_HEAD_DIM, 1)
        v = z[:, LANES:]
        v_sw = pltpu.roll(v, ATT_HEAD_DIM, 1)
        kv_s[BLOCK:, 0 * LANES:1 * LANES] = jnp.where(lo_half, k, k_sw).astype(BF16)
        kv_s[BLOCK:, 1 * LANES:2 * LANES] = jnp.where(lo_half, k_sw, k).astype(BF16)
        kv_s[BLOCK:, 2 * LANES:3 * LANES] = jnp.where(lo_half, v, 1.0).astype(BF16)
        kv_s[BLOCK:, 3 * LANES:4 * LANES] = jnp.where(lo_half, v_sw, 1.0).astype(BF16)

    def gv_piece(lo, width):
        gv_s[:, lo:lo + width] = proj(C_GV + lo, width).astype(BF16)

    def sg_piece(lo, width):
        z = proj(C_ZA + lo, width)
        sg_s[:, lo:lo + width] = (z * _sigmoid(z)).astype(BF16)

    def gate_low():
        glow_s[...] = proj(C_GL, LANES).astype(BF16)

    def gate_logsig():
        logit = _dot(glow_s[...], wup_ref[...]) + bg_ref[...]
        lg = (jnp.minimum(logit, 0.0) - jnp.log(1.0 + jnp.exp(-jnp.abs(logit)))) * (1.0 / GLA_GATE_NORMALIZER)
        lg_hi = lg.astype(BF16)
        lg_s[0] = lg_hi
        lg_s[1] = (lg - lg_hi.astype(F32)).astype(BF16)

    def gate_cumsum():
        ltri = ltri_ref[...]
        nb = CUM_BLK // GLA_CHUNK
        chunk_totals = []
        for r in range(0, tm, CUM_BLK):
            b = _dot(ltri, lg_s[0, r:r + CUM_BLK, :]) + _dot(ltri, lg_s[1, r:r + CUM_BLK, :])
            b3 = b.reshape(nb, GLA_CHUNK, GLA_QK_WIDTH)
            b_last = b3[:, GLA_CHUNK - 1:GLA_CHUNK, :]
            b_s[0, r:r + CUM_BLK, :] = b
            b_s[1, r:r + CUM_BLK, :] = (b_last - b3).reshape(CUM_BLK, GLA_QK_WIDTH)
            chunk_totals.append(b_last.reshape(nb, GLA_QK_WIDTH))
        pad = jnp.zeros((LANES - tm // GLA_CHUNK, GLA_QK_WIDTH), F32)
        decay = jnp.exp(jnp.concatenate(chunk_totals + [pad], axis=0))
        for h in range(GLA_HEADS):
            kc = slice(h * GLA_K_DIM, (h + 1) * GLA_K_DIM)
            dk_s[kc, :] = decay[:, kc].T

    def gq_piece(lo, width):
        z = proj(C_GQ + lo, width) * (GLA_K_DIM ** -0.5)
        qd_s[:, lo:lo + width] = (z * jnp.exp(b_s[0, :, lo:lo + width])).astype(BF16)

    def gk_piece(lo, width):
        z = proj(C_GK + lo, width)
        ki_s[:, lo:lo + width] = (z * jnp.exp(-b_s[0, :, lo:lo + width])).astype(BF16)
        kt_s[:, lo:lo + width] = (z * jnp.exp(b_s[1, :, lo:lo + width])).astype(BF16)

    column_pieces = {"q": q_piece, "kv": kv_piece, "gv": gv_piece, "sg": sg_piece, "gq": gq_piece, "gk": gk_piece}
    single_pieces = {"norm": norm, "gate_low": gate_low, "gate_logsig": gate_logsig, "gate_cumsum": gate_cumsum}

    def run(spec):
        if spec[0] in single_pieces:
            single_pieces[spec[0]]()
        else:
            column_pieces[spec[0]](spec[1], spec[2])
    return run


_INPROJ_ROUNDS = (
    ((("q", 0, 512), ("q", 512, 512)), (("kv", 0, 256), ("gate_low",))),
    ((("gv", 0, 512), ("gv", 512, 512)), (("gate_logsig",),)),
    ((("sg", 0, 512), ("sg", 512, 512)), (("gate_cumsum",),)),
    ((("sg", 1024, 512), ("gq", 0, 512)), (("sg", 1536, 512), ("gk", 0, 512))),
)


def _attn_scores(j, g, q_s, kv_s, qstack_s, score_s):
    col = lax.broadcasted_iota(jnp.int32, (BLOCK, BLOCK), 1)
    lo_half = col < ATT_HEAD_DIM
    zero = jnp.zeros((), BF16)
    kcat = kv_s[j * BLOCK:(j + 2) * BLOCK, g * LANES:(g + 1) * LANES]
    for pr in range(ATT_GROUP // 2):
        c0 = (g * (ATT_GROUP // 2) + pr) * LANES
        qp = q_s[j * BLOCK:(j + 1) * BLOCK, c0:c0 + LANES]
        qstack_s[(2 * pr) * BLOCK:(2 * pr + 1) * BLOCK, :] = jnp.where(lo_half, qp, zero)
        qstack_s[(2 * pr + 1) * BLOCK:(2 * pr + 2) * BLOCK, :] = jnp.where(lo_half, zero, qp)
    score_s[...] = _dot_nt(qstack_s[...], kcat)


def _attn_softmax(g, prev_bias, sinks_ref, score_s, p_s):
    row = lax.broadcasted_iota(jnp.int32, (BLOCK, BLOCK), 0)
    col = lax.broadcasted_iota(jnp.int32, (BLOCK, BLOCK), 1)
    causal = col <= row
    sink_terms = []
    for h in range(ATT_GROUP):
        rows = slice(h * BLOCK, (h + 1) * BLOCK)
        s_prev = score_s[rows, :BLOCK]
        if prev_bias is not None:
            s_prev = s_prev + prev_bias
        sf = jnp.where(causal, score_s[rows, BLOCK:], s_prev)
        sink = sinks_ref[g * ATT_GROUP + h]
        m = jnp.maximum(jnp.max(sf, axis=-1, keepdims=True), sink)
        e = jnp.exp(sf - m)
        sink_terms.append(jnp.exp(sink - m))
        p_s[rows, :BLOCK] = jnp.where(causal, 0.0, e).astype(BF16)
        p_s[rows, BLOCK:] = jnp.where(causal, e, 0.0).astype(BF16)
    return sink_terms


def _attn_values(j, g, sink_terms, p_s, kv_s, sg_s, y_ref):
    col = lax.broadcasted_iota(jnp.int32, (BLOCK, BLOCK), 1)
    lo_half = col < ATT_HEAD_DIM
    rows = slice(j * BLOCK, (j + 1) * BLOCK)
    vcat = kv_s[j * BLOCK:(j + 2) * BLOCK, (2 + g) * LANES:(3 + g) * LANES]
    o = _dot(p_s[...], vcat)
    for pr in range(ATT_GROUP // 2):
        he, ho = 2 * pr, 2 * pr + 1
        oe = o[he * BLOCK:(he + 1) * BLOCK]
        oo = o[ho * BLOCK:(ho + 1) * BLOCK]
        num = jnp.where(lo_half, oe, pltpu.roll(oo, ATT_HEAD_DIM, 1))
        den = jnp.where(lo_half, pltpu.roll(oe, ATT_HEAD_DIM, 1), oo) \
            + jnp.where(lo_half, sink_terms[he], sink_terms[ho])
        c0 = (g * (ATT_GROUP // 2) + pr) * LANES
        y = num * (1.0 / den) * sg_s[rows, c0:c0 + LANES].astype(F32)
        y_ref[rows, c0:c0 + LANES] = y.astype(BF16)


def _gla_independent(c, qd_s, ki_s, kt_s, gv_s):
    sl = slice(c * GLA_CHUNK, (c + 1) * GLA_CHUNK)
    out = []
    for h in range(GLA_HEADS):
        kc = slice(h * GLA_K_DIM, (h + 1) * GLA_K_DIM)
        vc = slice(h * GLA_V_DIM, (h + 1) * GLA_V_DIM)
        out.append((_dot_nt(qd_s[sl, kc], ki_s[sl, kc]), _dot_tn(kt_s[sl, kc], gv_s[sl, vc])))
    return out


def _gla_dependent(c, indep, qd_s, gv_s, sg_s, dk_s, gn_ref, st_s, y_ref):
    row = lax.broadcasted_iota(jnp.int32, (GLA_CHUNK, GLA_CHUNK), 0)
    col = lax.broadcasted_iota(jnp.int32, (GLA_CHUNK, GLA_CHUNK), 1)
    tril = col <= row
    gn = gn_ref[...]
    sl = slice(c * GLA_CHUNK, (c + 1) * GLA_CHUNK)
    for h in range(GLA_HEADS):
        kc = slice(h * GLA_K_DIM, (h + 1) * GLA_K_DIM)
        vc = slice(h * GLA_V_DIM, (h + 1) * GLA_V_DIM)
        oc = slice(ATT_WIDTH + h * GLA_V_DIM, ATT_WIDTH + (h + 1) * GLA_V_DIM)
        scores, kv_inc = indep[h]
        st = st_s[h]
        a = jnp.where(tril, scores, 0.0).astype(BF16)
        o = _dot(jnp.concatenate([qd_s[sl, kc], a], axis=1),
                 jnp.concatenate([st.astype(BF16), gv_s[sl, vc]], axis=0))
        st_s[h] = st * dk_s[kc, c:c + 1] + kv_inc
        y = _rms(o, gn) * sg_s[sl, oc].astype(F32)
        y_ref[sl, oc] = y.astype(BF16)


def _mixer_kernel(sinks_ref, x_ref, pos_ref, nmix_ref, w_ref, wup_ref, bg_ref, rope_ref, ltri_ref, gn_ref,
                  y_ref, u_s, trig_s, glow_s, lg_s, b_s, qstack_s, score_s, p_s,
                  q_s, kv_s, qd_s, ki_s, kt_s, gv_s, sg_s, dk_s, st_s, *,
                  tiles_per_seq, n_tiles):
    i = pl.program_id(0)
    tm = x_ref.shape[0]
    first = lax.rem(i - 1, tiles_per_seq) == 0

    @pl.when(first)
    def _():
        st_s[...] = jnp.zeros_like(st_s)

    def step(wr, do_inproj, do_mix):
        rd = 1 - wr
        inproj = _inproj_pieces(
            x_ref, pos_ref, nmix_ref, w_ref, wup_ref, bg_ref, rope_ref, ltri_ref, u_s, trig_s, glow_s, lg_s, b_s,
            q_s.at[wr], kv_s.at[wr], qd_s.at[wr], ki_s.at[wr], kt_s.at[wr], gv_s.at[wr], sg_s.at[wr],
            dk_s.at[wr])
        q_r, kv_r, sg_r, gv_r, qd_r = q_s.at[rd], kv_s.at[rd], sg_s.at[rd], gv_s.at[rd], qd_s.at[rd]
        first_bias = jnp.where(first, -jnp.inf, 0.0).astype(F32)

        if do_inproj:
            inproj(("norm",))
        for r, (long_group, short_group) in enumerate(_INPROJ_ROUNDS):
            j, g = divmod(r, ATT_KV_HEADS)
            if do_mix:
                _attn_scores(j, g, q_r, kv_r, qstack_s, score_s)
                sink_terms = _attn_softmax(g, first_bias if j == 0 else None, sinks_ref, score_s, p_s)
                gla_indep = _gla_independent(r, qd_r, ki_s.at[rd], kt_s.at[rd], gv_r)
            if do_inproj:
                for spec in long_group:
                    inproj(spec)
            if do_mix:
                _attn_values(j, g, sink_terms, p_s, kv_r, sg_r, y_ref)
                _gla_dependent(r, gla_indep, qd_r, gv_r, sg_r, dk_s.at[rd], gn_ref, st_s, y_ref)
            if do_inproj:
                for spec in short_group:
                    inproj(spec)
        if do_inproj:
            kv_s[wr, 0:BLOCK, :] = (kv_s[rd, tm:tm + BLOCK, :] if do_mix
                                    else jnp.zeros((BLOCK, kv_s.shape[2]), kv_s.dtype))

    assert len(_INPROJ_ROUNDS) == (tm // BLOCK) * ATT_KV_HEADS == tm // GLA_CHUNK
    is_first, is_last = i == 0, i == n_tiles
    pl.when(is_first)(functools.partial(step, 0, True, False))
    pl.when(is_last)(functools.partial(step, n_tiles % 2, False, True))
    for wr in range(2):
        pl.when((lax.rem(i, 2) == wr) & ~is_first & ~is_last)(functools.partial(step, wr, True, True))


def _wprep_source_row(c):
    n_head = C_GL // W_BLK
    tail_src = C_GL + GLA_GATE_RANK
    return jnp.where(c <= n_head, c * W_BLK, tail_src + (c - n_head - 1) * W_BLK)


def _wprep_kernel(*refs):
    *wt_refs, o_ref = refs
    for k, wt_ref in enumerate(wt_refs):
        c = pl.program_id(0) * WPREP_BLKS + k
        feat = _wprep_source_row(c) + lax.broadcasted_iota(jnp.int32, (W_BLK, 1), 0)
        scale = jnp.where(feat < ATT_WIDTH, ATT_HEAD_DIM ** -0.5, 1.0)
        padding = (c == C_GL // W_BLK) & (feat >= C_GL + GLA_GATE_RANK)
        o_ref[:, k * W_BLK:(k + 1) * W_BLK] = jnp.where(padding, 0.0, wt_ref[...] * scale).T.astype(BF16)


def _out_kernel(x_ref, y_ref, p_ref, wo_ref, pn_ref, wpg_ref, wpp_ref, fn_ref, o_ref, wo_s, wpg_s, wpp_s):
    @pl.when(pl.program_id(0) == 0)
    def _():
        wo_s[...] = wo_ref[...].astype(BF16)
        wpg_s[...] = wpg_ref[...].astype(BF16)
        wpp_s[...] = wpp_ref[...].astype(BF16)

    tm = x_ref.shape[0]
    halves = [slice(k * (tm // 2), (k + 1) * (tm // 2)) for k in range(2)]
    hs = [x_ref[r, :] + _dot(y_ref[r, :], wo_s[...]) for r in halves]
    ples = [_dot(p_ref[r, :].astype(BF16), wpp_s[...]) for r in halves]
    gates = [_dot(_rms(h, pn_ref[...]).astype(BF16), wpg_s[...]) for h in hs]
    for r, h, gate, ple in zip(halves, hs, gates, ples):
        o_ref[r, :] = _rms(h + _sigmoid(gate) * ple, fn_ref[...])


def _const_spec(shape):
    nd = len(shape)
    return pl.BlockSpec(shape, lambda *_: (0,) * nd, pipeline_mode=pl.Buffered(1))


def _rope_freqs():
    assert ROPE_HALF == SUBLANES
    inv_freq = ROPE_THETA ** (-jnp.arange(0, ROPE_DIM, 2, dtype=F32) / ROPE_DIM)
    return jnp.broadcast_to(inv_freq[:, None], (ROPE_HALF, LANES))


def _chunk_tril(n):
    i = np.arange(n)
    same = (i[:, None] // GLA_CHUNK) == (i[None, :] // GLA_CHUNK)
    return (same & (i[None, :] <= i[:, None])).astype(np.float32)


def _layer(h, p, pos, norm_mix, w_in, sinks, w_gate_up, b_gate, gla_norm, w_out, ple_norm,
           w_ple_gate, w_ple_proj, out_norm, batch, seq):
    n = batch * seq
    assert w_in.shape == (D_MODEL, D_IN_PROJ)
    w_t = jnp.swapaxes(w_in, 0, 1)
    src_spec = lambda k: pl.BlockSpec(
        (pl.Element(W_BLK), pl.Element(D_MODEL)),
        lambda c: (pl.multiple_of(_wprep_source_row(c * WPREP_BLKS + k), SUBLANES), 0))
    w_r = pl.pallas_call(
        _wprep_kernel,
        grid=(C_END // (WPREP_BLKS * W_BLK),),
        in_specs=[src_spec(k) for k in range(WPREP_BLKS)],
        out_specs=pl.BlockSpec((D_MODEL, WPREP_BLKS * W_BLK), lambda c: (0, c)),
        out_shape=jax.ShapeDtypeStruct((D_MODEL, C_END), BF16),
        compiler_params=pltpu.CompilerParams(dimension_semantics=("parallel",)),
        name="wprep",
    )(*([w_t] * WPREP_BLKS))
    wup = jnp.concatenate([w_gate_up, jnp.zeros((LANES - GLA_GATE_RANK, GLA_QK_WIDTH), w_gate_up.dtype)],
                          axis=0).astype(BF16)
    rope_freqs = _rope_freqs()
    ltri = jnp.asarray(_chunk_tril(CUM_BLK), dtype=BF16)

    params = lambda sem: pltpu.CompilerParams(dimension_semantics=sem, vmem_limit_bytes=VMEM_LIMIT)

    nt = n // TM_MIX
    in_tile = lambda i: (jnp.minimum(i, nt - 1), 0)
    out_tile = lambda i: (jnp.maximum(i - 1, 0), 0)
    two = lambda *shape: (2,) + shape
    y = pl.pallas_call(
        functools.partial(_mixer_kernel, tiles_per_seq=seq // TM_MIX, n_tiles=nt),
        grid=(nt + 1,),
        in_specs=[
            pl.BlockSpec(memory_space=pltpu.SMEM),
            pl.BlockSpec((TM_MIX, D_MODEL), in_tile),
            pl.BlockSpec((1, TM_MIX // LANES, LANES), lambda i: (jnp.minimum(i, nt - 1), 0, 0)),
            _const_spec((1, D_MODEL)),
            _const_spec((D_MODEL, C_END)),
            _const_spec((LANES, GLA_QK_WIDTH)),
            _const_spec((1, GLA_QK_WIDTH)),
            _const_spec((SUBLANES, LANES)),
            _const_spec((CUM_BLK, CUM_BLK)),
            _const_spec((1, GLA_V_DIM)),
        ],
        out_specs=pl.BlockSpec((TM_MIX, ATT_WIDTH + GLA_WIDTH), out_tile),
        out_shape=jax.ShapeDtypeStruct((n, ATT_WIDTH + GLA_WIDTH), BF16),
        scratch_shapes=[
            pltpu.VMEM((TM_MIX, D_MODEL), BF16),
            pltpu.VMEM((3, TM_MIX, LANES), F32),
            pltpu.VMEM((TM_MIX, LANES), BF16),
            pltpu.VMEM((2, TM_MIX, GLA_QK_WIDTH), BF16),
            pltpu.VMEM((2, TM_MIX, GLA_QK_WIDTH), F32),
            pltpu.VMEM((ATT_GROUP * BLOCK, LANES), BF16),
            pltpu.VMEM((ATT_GROUP * BLOCK, 2 * BLOCK), F32),
            pltpu.VMEM((ATT_GROUP * BLOCK, 2 * BLOCK), BF16),
            pltpu.VMEM(two(TM_MIX, ATT_WIDTH), BF16),
            pltpu.VMEM(two(TM_MIX + BLOCK, 4 * ATT_KV_WIDTH), BF16),
            pltpu.VMEM(two(TM_MIX, GLA_QK_WIDTH), BF16),
            pltpu.VMEM(two(TM_MIX, GLA_QK_WIDTH), BF16),
            pltpu.VMEM(two(TM_MIX, GLA_QK_WIDTH), BF16),
            pltpu.VMEM(two(TM_MIX, GLA_WIDTH), BF16),
            pltpu.VMEM(two(TM_MIX, ATT_WIDTH + GLA_WIDTH), BF16),
            pltpu.VMEM(two(GLA_QK_WIDTH, LANES), F32),
            pltpu.VMEM((GLA_HEADS, GLA_K_DIM, GLA_V_DIM), F32),
        ],
        compiler_params=params(("arbitrary",)),
        name="mixer",
    )(sinks.astype(F32), h, pos.reshape(nt, TM_MIX // LANES, LANES), norm_mix.reshape(1, D_MODEL), w_r, wup,
      b_gate.reshape(1, GLA_QK_WIDTH), rope_freqs, ltri, gla_norm.reshape(1, GLA_V_DIM))

    row = lambda i: (i, 0)
    return pl.pallas_call(
        _out_kernel,
        grid=(n // TM_OUT,),
        in_specs=[
            pl.BlockSpec((TM_OUT, D_MODEL), row),
            pl.BlockSpec((TM_OUT, ATT_WIDTH + GLA_WIDTH), row),
            pl.BlockSpec((TM_OUT, D_PLE), row),
            _const_spec((ATT_WIDTH + GLA_WIDTH, D_MODEL)),
            _const_spec((1, D_MODEL)),
            _const_spec((D_MODEL, D_MODEL)),
            _const_spec((D_PLE, D_MODEL)),
            _const_spec((1, D_MODEL)),
        ],
        out_specs=pl.BlockSpec((TM_OUT, D_MODEL), row),
        out_shape=jax.ShapeDtypeStruct((n, D_MODEL), F32),
        scratch_shapes=[
            pltpu.VMEM((ATT_WIDTH + GLA_WIDTH, D_MODEL), BF16),
            pltpu.VMEM((D_MODEL, D_MODEL), BF16),
            pltpu.VMEM((D_PLE, D_MODEL), BF16),
        ],
        compiler_params=params(("arbitrary",)),
        name="outproj",
    )(h, y, p, w_out, ple_norm.reshape(1, D_MODEL), w_ple_gate, w_ple_proj, out_norm.reshape(1, D_MODEL))


def kernel(x, p, positions, norm_mix, w_in, attn_sinks, w_gate_up, b_gate, gla_norm, w_out, ple_norm,
           w_ple_gate, w_ple_proj, final_norm):
    batch, seq, _ = x.shape
    depth = w_in.shape[0]
    assert depth == 1, "the final RMSNorm is fused into the single layer's output kernel"
    n = batch * seq
    layer = lambda a: a.reshape(a.shape[1:])
    out = _layer(x.reshape(n, D_MODEL), p.reshape(n, D_PLE), positions.reshape(n), layer(norm_mix), layer(w_in),
                 layer(attn_sinks), layer(w_gate_up), layer(b_gate), layer(gla_norm), layer(w_out),
                 layer(ple_norm), layer(w_ple_gate), layer(w_ple_proj), final_norm, batch, seq)
    return out.reshape(batch, seq, D_MODEL)
```

```python
import functools

import jax
import jax.numpy as jnp
import numpy as np
from jax import lax
from jax.experimental import pallas as pl
from jax.experimental.pallas import tpu as pltpu

D_MODEL = 1024
D_PLE = 256
ATT_HEADS = 16
ATT_KV_HEADS = 2
ATT_HEAD_DIM = 64
ATT_WIDTH = ATT_HEADS * ATT_HEAD_DIM
ATT_KV_WIDTH = ATT_KV_HEADS * ATT_HEAD_DIM
ATT_GROUP = ATT_HEADS // ATT_KV_HEADS
BLOCK = 128
ROPE_DIM = ATT_HEAD_DIM // 4
ROPE_HALF = ROPE_DIM // 2
ROPE_THETA = 500000.0
GLA_HEADS = 4
GLA_V_DIM = 256
GLA_K_DIM = 128
GLA_WIDTH = GLA_HEADS * GLA_V_DIM
GLA_QK_WIDTH = GLA_HEADS * GLA_K_DIM
GLA_GATE_RANK = 16
GLA_GATE_NORMALIZER = 16.0
GLA_CHUNK = 64
EPS = 1e-6

LANES = 128
SUBLANES = 8
VMEM_LIMIT = 56 * 1024 * 1024

W_BLK = 256
WPREP_BLKS = 11

C_Q = 0
C_KV = C_Q + ATT_WIDTH
C_GQ = C_KV + 2 * ATT_KV_WIDTH
C_GK = C_GQ + GLA_QK_WIDTH
C_GV = C_GK + GLA_QK_WIDTH
C_GL = C_GV + GLA_WIDTH
C_ZA = C_GL + W_BLK
C_ZG = C_ZA + ATT_WIDTH
C_END = C_ZG + GLA_WIDTH
D_IN_PROJ = C_END - W_BLK + GLA_GATE_RANK

TM_MIX = 256
TM_OUT = 256
CUM_BLK = 256

BF16 = jnp.bfloat16
F32 = jnp.float32


def _dot(a, b):
    return jnp.dot(a, b, preferred_element_type=F32)


def _dot_nt(a, b):
    return lax.dot_general(a, b, (((1,), (1,)), ((), ())), preferred_element_type=F32)


def _dot_tn(a, b):
    return lax.dot_general(a, b, (((0,), (0,)), ((), ())), preferred_element_type=F32)


def _rms(x, g):
    return x * lax.rsqrt(jnp.mean(x * x, axis=-1, keepdims=True) + EPS) * g


def _sigmoid(z):
    return 0.5 * (1.0 + jnp.tanh(0.5 * z))


def _inproj_pieces(x_ref, pos_ref, nmix_ref, w_ref, wup_ref, bg_ref, rope_ref, ltri_ref,
                   u_s, trig_s, glow_s, lg_s, b_s, q_s, kv_s, qd_s, ki_s, kt_s, gv_s, sg_s, dk_s):
    tm = x_ref.shape[0]

    def proj(c0, width):
        return _dot(u_s[...], w_ref[:, c0:c0 + width])

    def norm():
        u_s[...] = _rms(x_ref[...], nmix_ref[...]).astype(BF16)
        pos = pos_ref[0].astype(F32)
        ones = jnp.ones((ATT_HEAD_DIM - ROPE_DIM, LANES), F32)
        zeros = jnp.zeros((ATT_HEAD_DIM - ROPE_DIM, LANES), F32)
        zero8 = jnp.zeros((ROPE_HALF, LANES), F32)
        for r in range(tm // LANES):
            ang = rope_ref[...] * pos[r:r + 1, :]
            c8 = jnp.cos(ang)
            s8 = jnp.sin(ang)
            rows = slice(r * LANES, (r + 1) * LANES)
            trig_s[0, rows, :] = jnp.concatenate([c8, c8, ones] * 2, axis=0).T
            trig_s[1, rows, :] = jnp.concatenate([-s8, zero8, zeros] * 2, axis=0).T
            trig_s[2, rows, :] = jnp.concatenate([zero8, s8, zeros] * 2, axis=0).T

    def rope(z):
        up = pltpu.roll(z, LANES - ROPE_HALF, 1)
        dn = pltpu.roll(z, ROPE_HALF, 1)
        return z * trig_s[0] + up * trig_s[1] + dn * trig_s[2]

    def q_piece(lo, width):
        z = proj(C_Q + lo, width)
        for l in range(0, width, LANES):
            q_s[:, lo + l:lo + l + LANES] = rope(z[:, l:l + LANES]).astype(BF16)

    def kv_piece(lo, width):
        assert (lo, width) == (0, 2 * ATT_KV_WIDTH)
        z = proj(C_KV, width)
        lo_half = lax.broadcasted_iota(jnp.int32, (tm, LANES), 1) < ATT_HEAD_DIM
        k = rope(z[:, :LANES])
        k_sw = pltpu.roll(k, ATT_HEAD_DIM, 1)
        v = z[:, LANES:]
        v_sw = pltpu.roll(v, ATT_HEAD_DIM, 1)
        kv_s[BLOCK:, 0 * LANES:1 * LANES] = jnp.where(lo_half, k, k_sw).astype(BF16)
        kv_s[BLOCK:, 1 * LANES:2 * LANES] = jnp.where(lo_half, k_sw, k).astype(BF16)
        kv_s[BLOCK:, 2 * LANES:3 * LANES] = jnp.where(lo_half, v, 1.0).astype(BF16)
        kv_s[BLOCK:, 3 * LANES:4 * LANES] = jnp.where(lo_half, v_sw, 1.0).astype(BF16)

    def gv_piece(lo, width):
        gv_s[:, lo:lo + width] = proj(C_GV + lo, width).astype(BF16)

    def sg_piece(lo, width):
        z = proj(C_ZA + lo, width)
        sg_s[:, lo:lo + width] = (z * _sigmoid(z)).astype(BF16)

    def gate_low():
        glow_s[...] = proj(C_GL, LANES).astype(BF16)

    def gate_logsig():
        logit = _dot(glow_s[...], wup_ref[...]) + bg_ref[...]
        lg = (jnp.minimum(logit, 0.0) - jnp.log(1.0 + jnp.exp(-jnp.abs(logit)))) * (1.0 / GLA_GATE_NORMALIZER)
        lg_hi = lg.astype(BF16)
        lg_s[0] = lg_hi
        lg_s[1] = (lg - lg_hi.astype(F32)).astype(BF16)

    def gate_cumsum():
        ltri = ltri_ref[...]
        nb = CUM_BLK // GLA_CHUNK
        chunk_totals = []
        for r in range(0, tm, CUM_BLK):
            b = _dot(ltri, lg_s[0, r:r + CUM_BLK, :]) + _dot(ltri, lg_s[1, r:r + CUM_BLK, :])
            b3 = b.reshape(nb, GLA_CHUNK, GLA_QK_WIDTH)
            b_last = b3[:, GLA_CHUNK - 1:GLA_CHUNK, :]
            b_s[0, r:r + CUM_BLK, :] = b
            b_s[1, r:r + CUM_BLK, :] = (b_last - b3).reshape(CUM_BLK, GLA_QK_WIDTH)
            chunk_totals.append(b_last.reshape(nb, GLA_QK_WIDTH))
        pad = jnp.zeros((LANES - tm // GLA_CHUNK, GLA_QK_WIDTH), F32)
        decay = jnp.exp(jnp.concatenate(chunk_totals + [pad], axis=0))
        for h in range(GLA_HEADS):
            kc = slice(h * GLA_K_DIM, (h + 1) * GLA_K_DIM)
            dk_s[kc, :] = decay[:, kc].T

    def gq_piece(lo, width):
        z = proj(C_GQ + lo, width) * (GLA_K_DIM ** -0.5)
        qd_s[:, lo:lo + width] = (z * jnp.exp(b_s[0, :, lo:lo + width])).astype(BF16)

    def gk_piece(lo, width):
        z = proj(C_GK + lo, width)
        ki_s[:, lo:lo + width] = (z * jnp.exp(-b_s[0, :, lo:lo + width])).astype(BF16)
        kt_s[:, lo:lo + width] = (z * jnp.exp(b_s[1, :, lo:lo + width])).astype(BF16)

    column_pieces = {"q": q_piece, "kv": kv_piece, "gv": gv_piece, "sg": sg_piece, "gq": gq_piece, "gk": gk_piece}
    single_pieces = {"norm": norm, "gate_low": gate_low, "gate_logsig": gate_logsig, "gate_cumsum": gate_cumsum}

    def run(spec):
        if spec[0] in single_pieces:
            single_pieces[spec[0]]()
        else:
            column_pieces[spec[0]](spec[1], spec[2])
    return run


_INPROJ_ROUNDS = (
    ((("q", 0, 512), ("q", 512, 512)), (("kv", 0, 256), ("gate_low",))),
    ((("gv", 0, 512), ("gv", 512, 512)), (("gate_logsig",),)),
    ((("sg", 0, 512), ("sg", 512, 512)), (("gate_cumsum",),)),
    ((("sg", 1024, 512), ("gq", 0, 512)), (("sg", 1536, 512), ("gk", 0, 512))),
)


def _attn_scores(j, g, q_s, kv_s, qstack_s, score_s):
    col = lax.broadcasted_iota(jnp.int32, (BLOCK, BLOCK), 1)
    lo_half = col < ATT_HEAD_DIM
    zero = jnp.zeros((), BF16)
    kcat = kv_s[j * BLOCK:(j + 2) * BLOCK, g * LANES:(g + 1) * LANES]
    for pr in range(ATT_GROUP // 2):
        c0 = (g * (ATT_GROUP // 2) + pr) * LANES
        qp = q_s[j * BLOCK:(j + 1) * BLOCK, c0:c0 + LANES]
        qstack_s[(2 * pr) * BLOCK:(2 * pr + 1) * BLOCK, :] = jnp.where(lo_half, qp, zero)
        qstack_s[(2 * pr + 1) * BLOCK:(2 * pr + 2) * BLOCK, :] = jnp.where(lo_half, zero, qp)
    score_s[...] = _dot_nt(qstack_s[...], kcat)


def _attn_softmax(g, prev_bias, sinks_ref, score_s, p_s):
    row = lax.broadcasted_iota(jnp.int32, (BLOCK, BLOCK), 0)
    col = lax.broadcasted_iota(jnp.int32, (BLOCK, BLOCK), 1)
    causal = col <= row
    sink_terms = []
    for h in range(ATT_GROUP):
        rows = slice(h * BLOCK, (h + 1) * BLOCK)
        s_prev = score_s[rows, :BLOCK]
        if prev_bias is not None:
            s_prev = s_prev + prev_bias
        sf = jnp.where(causal, score_s[rows, BLOCK:], s_prev)
        sink = sinks_ref[g * ATT_GROUP + h]
        m = jnp.maximum(jnp.max(sf, axis=-1, keepdims=True), sink)
        e = jnp.exp(sf - m)
        sink_terms.append(jnp.exp(sink - m))
        p_s[rows, :BLOCK] = jnp.where(causal, 0.0, e).astype(BF16)
        p_s[rows, BLOCK:] = jnp.where(causal, e, 0.0).astype(BF16)
    return sink_terms


def _attn_values(j, g, sink_terms, p_s, kv_s, sg_s, y_ref):
    col = lax.broadcasted_iota(jnp.int32, (BLOCK, BLOCK), 1)
    lo_half = col < ATT_HEAD_DIM
    rows = slice(j * BLOCK, (j + 1) * BLOCK)
    vcat = kv_s[j * BLOCK:(j + 2) * BLOCK, (2 + g) * LANES:(3 + g) * LANES]
    o = _dot(p_s[...], vcat)
    for pr in range(ATT_GROUP // 2):
        he, ho = 2 * pr, 2 * pr + 1
        oe = o[he * BLOCK:(he + 1) * BLOCK]
        oo = o[ho * BLOCK:(ho + 1) * BLOCK]
        num = jnp.where(lo_half, oe, pltpu.roll(oo, ATT_HEAD_DIM, 1))
        den = jnp.where(lo_half, pltpu.roll(oe, ATT_HEAD_DIM, 1), oo) \
            + jnp.where(lo_half, sink_terms[he], sink_terms[ho])
        c0 = (g * (ATT_GROUP // 2) + pr) * LANES
        y = num * (1.0 / den) * sg_s[rows, c0:c0 + LANES].astype(F32)
        y_ref[rows, c0:c0 + LANES] = y.astype(BF16)


def _gla_independent(c, qd_s, ki_s, kt_s, gv_s):
    sl = slice(c * GLA_CHUNK, (c + 1) * GLA_CHUNK)
    out = []
    for h in range(GLA_HEADS):
        kc = slice(h * GLA_K_DIM, (h + 1) * GLA_K_DIM)
        vc = slice(h * GLA_V_DIM, (h + 1) * GLA_V_DIM)
        out.append((_dot_nt(qd_s[sl, kc], ki_s[sl, kc]), _dot_tn(kt_s[sl, kc], gv_s[sl, vc])))
    return out


def _gla_dependent(c, indep, qd_s, gv_s, sg_s, dk_s, gn_ref, st_s, y_ref):
    row = lax.broadcasted_iota(jnp.int32, (GLA_CHUNK, GLA_CHUNK), 0)
    col = lax.broadcasted_iota(jnp.int32, (GLA_CHUNK, GLA_CHUNK), 1)
    tril = col <= row
    gn = gn_ref[...]
    sl = slice(c * GLA_CHUNK, (c + 1) * GLA_CHUNK)
    for h in range(GLA_HEADS):
        kc = slice(h * GLA_K_DIM, (h + 1) * GLA_K_DIM)
        vc = slice(h * GLA_V_DIM, (h + 1) * GLA_V_DIM)
        oc = slice(ATT_WIDTH + h * GLA_V_DIM, ATT_WIDTH + (h + 1) * GLA_V_DIM)
        scores, kv_inc = indep[h]
        st = st_s[h]
        a = jnp.where(tril, scores, 0.0).astype(BF16)
        o = _dot(jnp.concatenate([qd_s[sl, kc], a], axis=1),
                 jnp.concatenate([st.astype(BF16), gv_s[sl, vc]], axis=0))
        st_s[h] = st * dk_s[kc, c:c + 1] + kv_inc
        y = _rms(o, gn) * sg_s[sl, oc].astype(F32)
        y_ref[sl, oc] = y.astype(BF16)


def _mixer_kernel(sinks_ref, x_ref, pos_ref, nmix_ref, w_ref, wup_ref, bg_ref, rope_ref, ltri_ref, gn_ref,
                  y_ref, u_s, trig_s, glow_s, lg_s, b_s, qstack_s, score_s, p_s,
                  q_s, kv_s, qd_s, ki_s, kt_s, gv_s, sg_s, dk_s, st_s, *,
                  tiles_per_seq, n_tiles):
    i = pl.program_id(0)
    tm = x_ref.shape[0]
    first = lax.rem(i - 1, tiles_per_seq) == 0

    @pl.when(first)
    def _():
        st_s[...] = jnp.zeros_like(st_s)

    def step(wr, do_inproj, do_mix):
        rd = 1 - wr
        inproj = _inproj_pieces(
            x_ref, pos_ref, nmix_ref, w_ref, wup_ref, bg_ref, rope_ref, ltri_ref, u_s, trig_s, glow_s, lg_s, b_s,
            q_s.at[wr], kv_s.at[wr], qd_s.at[wr], ki_s.at[wr], kt_s.at[wr], gv_s.at[wr], sg_s.at[wr],
            dk_s.at[wr])
        q_r, kv_r, sg_r, gv_r, qd_r = q_s.at[rd], kv_s.at[rd], sg_s.at[rd], gv_s.at[rd], qd_s.at[rd]
        first_bias = jnp.where(first, -jnp.inf, 0.0).astype(F32)

        if do_inproj:
            inproj(("norm",))
        for r, (long_group, short_group) in enumerate(_INPROJ_ROUNDS):
            j, g = divmod(r, ATT_KV_HEADS)
            if do_mix:
                _attn_scores(j, g, q_r, kv_r, qstack_s, score_s)
                sink_terms = _attn_softmax(g, first_bias if j == 0 else None, sinks_ref, score_s, p_s)
                gla_indep = _gla_independent(r, qd_r, ki_s.at[rd], kt_s.at[rd], gv_r)
            if do_inproj:
                for spec in long_group:
                    inproj(spec)
            if do_mix:
                _attn_values(j, g, sink_terms, p_s, kv_r, sg_r, y_ref)
                _gla_dependent(r, gla_indep, qd_r, gv_r, sg_r, dk_s.at[rd], gn_ref, st_s, y_ref)
            if do_inproj:
                for spec in short_group:
                    inproj(spec)
        if do_inproj:
            kv_s[wr, 0:BLOCK, :] = (kv_s[rd, tm:tm + BLOCK, :] if do_mix
                                    else jnp.zeros((BLOCK, kv_s.shape[2]), kv_s.dtype))

    assert len(_INPROJ_ROUNDS) == (tm // BLOCK) * ATT_KV_HEADS == tm // GLA_CHUNK
    is_first, is_last = i == 0, i == n_tiles
    pl.when(is_first)(functools.partial(step, 0, True, False))
    pl.when(is_last)(functools.partial(step, n_tiles % 2, False, True))
    for wr in range(2):
        pl.when((lax.rem(i, 2) == wr) & ~is_first & ~is_last)(functools.partial(step, wr, True, True))


def _wprep_source_row(c):
    n_head = C_GL // W_BLK
    tail_src = C_GL + GLA_GATE_RANK
    return jnp.where(c <= n_head, c * W_BLK, tail_src + (c - n_head - 1) * W_BLK)


def _wprep_kernel(*refs):
    *wt_refs, o_ref = refs
    for k, wt_ref in enumerate(wt_refs):
        c = pl.program_id(0) * WPREP_BLKS + k
        feat = _wprep_source_row(c) + lax.broadcasted_iota(jnp.int32, (W_BLK, 1), 0)
        scale = jnp.where(feat < ATT_WIDTH, ATT_HEAD_DIM ** -0.5, 1.0)
        padding = (c == C_GL // W_BLK) & (feat >= C_GL + GLA_GATE_RANK)
        o_ref[:, k * W_BLK:(k + 1) * W_BLK] = jnp.where(padding, 0.0, wt_ref[...] * scale).T.astype(BF16)


def _out_kernel(x_ref, y_ref, p_ref, wo_ref, pn_ref, wpg_ref, wpp_ref, fn_ref, o_ref, wo_s, wpg_s, wpp_s):
    @pl.when(pl.program_id(0) == 0)
    def _():
        wo_s[...] = wo_ref[...].astype(BF16)
        wpg_s[...] = wpg_ref[...].astype(BF16)
        wpp_s[...] = wpp_ref[...].astype(BF16)

    tm = x_ref.shape[0]
    halves = [slice(0, tm)]
    hs = [x_ref[r, :] + _dot(y_ref[r, :], wo_s[...]) for r in halves]
    ples = [_dot(p_ref[r, :].astype(BF16), wpp_s[...]) for r in halves]
    gates = [_dot(_rms(h, pn_ref[...]).astype(BF16), wpg_s[...]) for h in hs]
    for r, h, gate, ple in zip(halves, hs, gates, ples):
        o_ref[r, :] = _rms(h + _sigmoid(gate) * ple, fn_ref[...])


def _const_spec(shape):
    nd = len(shape)
    return pl.BlockSpec(shape, lambda *_: (0,) * nd, pipeline_mode=pl.Buffered(1))


def _rope_freqs():
    assert ROPE_HALF == SUBLANES
    inv_freq = ROPE_THETA ** (-jnp.arange(0, ROPE_DIM, 2, dtype=F32) / ROPE_DIM)
    return jnp.broadcast_to(inv_freq[:, None], (ROPE_HALF, LANES))


def _chunk_tril(n):
    i = np.arange(n)
    same = (i[:, None] // GLA_CHUNK) == (i[None, :] // GLA_CHUNK)
    return (same & (i[None, :] <= i[:, None])).astype(np.float32)


def _layer(h, p, pos, norm_mix, w_in, sinks, w_gate_up, b_gate, gla_norm, w_out, ple_norm,
           w_ple_gate, w_ple_proj, out_norm, batch, seq):
    n = batch * seq
    assert w_in.shape == (D_MODEL, D_IN_PROJ)
    w_t = jnp.swapaxes(w_in, 0, 1)
    src_spec = lambda k: pl.BlockSpec(
        (pl.Element(W_BLK), pl.Element(D_MODEL)),
        lambda c: (pl.multiple_of(_wprep_source_row(c * WPREP_BLKS + k), SUBLANES), 0))
    w_r = pl.pallas_call(
        _wprep_kernel,
        grid=(C_END // (WPREP_BLKS * W_BLK),),
        in_specs=[src_spec(k) for k in range(WPREP_BLKS)],
        out_specs=pl.BlockSpec((D_MODEL, WPREP_BLKS * W_BLK), lambda c: (0, c)),
        out_shape=jax.ShapeDtypeStruct((D_MODEL, C_END), BF16),
        compiler_params=pltpu.CompilerParams(dimension_semantics=("parallel",)),
        name="wprep",
    )(*([w_t] * WPREP_BLKS))
    wup = jnp.concatenate([w_gate_up, jnp.zeros((LANES - GLA_GATE_RANK, GLA_QK_WIDTH), w_gate_up.dtype)],
                          axis=0).astype(BF16)
    rope_freqs = _rope_freqs()
    ltri = jnp.asarray(_chunk_tril(CUM_BLK), dtype=BF16)

    params = lambda sem: pltpu.CompilerParams(dimension_semantics=sem, vmem_limit_bytes=VMEM_LIMIT)

    nt = n // TM_MIX
    in_tile = lambda i: (jnp.minimum(i, nt - 1), 0)
    out_tile = lambda i: (jnp.maximum(i - 1, 0), 0)
    two = lambda *shape: (2,) + shape
    y = pl.pallas_call(
        functools.partial(_mixer_kernel, tiles_per_seq=seq // TM_MIX, n_tiles=nt),
        grid=(nt + 1,),
        in_specs=[
            pl.BlockSpec(memory_space=pltpu.SMEM),
            pl.BlockSpec((TM_MIX, D_MODEL), in_tile),
            pl.BlockSpec((1, TM_MIX // LANES, LANES), lambda i: (jnp.minimum(i, nt - 1), 0, 0)),
            _const_spec((1, D_MODEL)),
            _const_spec((D_MODEL, C_END)),
            _const_spec((LANES, GLA_QK_WIDTH)),
            _const_spec((1, GLA_QK_WIDTH)),
            _const_spec((SUBLANES, LANES)),
            _const_spec((CUM_BLK, CUM_BLK)),
            _const_spec((1, GLA_V_DIM)),
        ],
        out_specs=pl.BlockSpec((TM_MIX, ATT_WIDTH + GLA_WIDTH), out_tile),
        out_shape=jax.ShapeDtypeStruct((n, ATT_WIDTH + GLA_WIDTH), BF16),
        scratch_shapes=[
            pltpu.VMEM((TM_MIX, D_MODEL), BF16),
            pltpu.VMEM((3, TM_MIX, LANES), F32),
            pltpu.VMEM((TM_MIX, LANES), BF16),
            pltpu.VMEM((2, TM_MIX, GLA_QK_WIDTH), BF16),
            pltpu.VMEM((2, TM_MIX, GLA_QK_WIDTH), F32),
            pltpu.VMEM((ATT_GROUP * BLOCK, LANES), BF16),
            pltpu.VMEM((ATT_GROUP * BLOCK, 2 * BLOCK), F32),
            pltpu.VMEM((ATT_GROUP * BLOCK, 2 * BLOCK), BF16),
            pltpu.VMEM(two(TM_MIX, ATT_WIDTH), BF16),
            pltpu.VMEM(two(TM_MIX + BLOCK, 4 * ATT_KV_WIDTH), BF16),
            pltpu.VMEM(two(TM_MIX, GLA_QK_WIDTH), BF16),
            pltpu.VMEM(two(TM_MIX, GLA_QK_WIDTH), BF16),
            pltpu.VMEM(two(TM_MIX, GLA_QK_WIDTH), BF16),
            pltpu.VMEM(two(TM_MIX, GLA_WIDTH), BF16),
            pltpu.VMEM(two(TM_MIX, ATT_WIDTH + GLA_WIDTH), BF16),
            pltpu.VMEM(two(GLA_QK_WIDTH, LANES), F32),
            pltpu.VMEM((GLA_HEADS, GLA_K_DIM, GLA_V_DIM), F32),
        ],
        compiler_params=params(("arbitrary",)),
        name="mixer",
    )(sinks.astype(F32), h, pos.reshape(nt, TM_MIX // LANES, LANES), norm_mix.reshape(1, D_MODEL), w_r, wup,
      b_gate.reshape(1, GLA_QK_WIDTH), rope_freqs, ltri, gla_norm.reshape(1, GLA_V_DIM))

    row = lambda i: (i, 0)
    return pl.pallas_call(
        _out_kernel,
        grid=(n // TM_OUT,),
        in_specs=[
            pl.BlockSpec((TM_OUT, D_MODEL), row),
            pl.BlockSpec((TM_OUT, ATT_WIDTH + GLA_WIDTH), row),
            pl.BlockSpec((TM_OUT, D_PLE), row),
            _const_spec((ATT_WIDTH + GLA_WIDTH, D_MODEL)),
            _const_spec((1, D_MODEL)),
            _const_spec((D_MODEL, D_MODEL)),
            _const_spec((D_PLE, D_MODEL)),
            _const_spec((1, D_MODEL)),
        ],
        out_specs=pl.BlockSpec((TM_OUT, D_MODEL), row),
        out_shape=jax.ShapeDtypeStruct((n, D_MODEL), F32),
        scratch_shapes=[
            pltpu.VMEM((ATT_WIDTH + GLA_WIDTH, D_MODEL), BF16),
            pltpu.VMEM((D_MODEL, D_MODEL), BF16),
            pltpu.VMEM((D_PLE, D_MODEL), BF16),
        ],
        compiler_params=params(("arbitrary",)),
        name="outproj",
    )(h, y, p, w_out, ple_norm.reshape(1, D_MODEL), w_ple_gate, w_ple_proj, out_norm.reshape(1, D_MODEL))


def kernel(x, p, positions, norm_mix, w_in, attn_sinks, w_gate_up, b_gate, gla_norm, w_out, ple_norm,
           w_ple_gate, w_ple_proj, final_norm):
    batch, seq, _ = x.shape
    depth = w_in.shape[0]
    assert depth == 1, "the final RMSNorm is fused into the single layer's output kernel"
    n = batch * seq
    layer = lambda a: a.reshape(a.shape[1:])
    out = _layer(x.reshape(n, D_MODEL), p.reshape(n, D_PLE), positions.reshape(n), layer(norm_mix), layer(w_in),
                 layer(attn_sinks), layer(w_gate_up), layer(b_gate), layer(gla_norm), layer(w_out),
                 layer(ple_norm), layer(w_ple_gate), layer(w_ple_proj), final_norm, batch, seq)
    return out.reshape(batch, seq, D_MODEL)
```
